```python
import math
import jax, jax.numpy as jnp
from jax import lax
import numpy as np

D_MODEL = 1024
BATCH = 8
SEQ = 4096
DEPTH = 2

CHUNK = 64
POOL_W = 256
POOL_WINDOWS = (2, 4, 8, 16)
POOL_GROUPS = len(POOL_WINDOWS)
POOL_GC = POOL_W // POOL_GROUPS
CONV_W = 256
CONV_K = 31
N_HEADS = 4
QK_NOPE = 128
QK_ROPE = 64
QK_HEAD = QK_NOPE + QK_ROPE
V_DIM = 128
Q_LORA = 256
KV_LORA = 128
ATTN_W = N_HEADS * V_DIM
ROPE_THETA = 10000.0
Q_BLOCK = 128
D_MIX = POOL_W + CONV_W + ATTN_W
D_IN = POOL_W + 2 * CONV_W + Q_LORA + KV_LORA + QK_ROPE
N_EXPERTS = 32
TOP_K = 4
D_FF = D_MODEL
SWIGLU_LIMIT = 7.0
SWIGLU_ALPHA = 1.702
EXPERT_BLOCK = 256
EPS = 1e-6

kernel_name = "hybrid_pool_conv_mla_moe_chunk_causal"


def rmsnorm(x, g):
    xf = x.astype(jnp.float32)
    y = xf * lax.rsqrt(jnp.mean(xf * xf, axis=-1, keepdims=True) + EPS)
    return (y * g.astype(jnp.float32)).astype(x.dtype)


def layernorm(x, g, b):
    xf = x.astype(jnp.float32)
    mu = jnp.mean(xf, axis=-1, keepdims=True)
    var = jnp.mean(jnp.square(xf - mu), axis=-1, keepdims=True)
    y = (xf - mu) * lax.rsqrt(var + EPS) * g.astype(jnp.float32) + b.astype(jnp.float32)
    return y.astype(x.dtype)


def rope_tables(positions):
    inv_freq = 1.0 / (ROPE_THETA ** (jnp.arange(0, QK_ROPE, 2, dtype=jnp.float32) / QK_ROPE))
    ang = positions.astype(jnp.float32)[..., None] * inv_freq
    return jnp.cos(ang), jnp.sin(ang)


def apply_rope(x, cos, sin):
    xf = x.astype(jnp.float32)
    x1, x2 = xf[..., : QK_ROPE // 2], xf[..., QK_ROPE // 2:]
    c, s = cos[:, :, None, :], sin[:, :, None, :]
    return jnp.concatenate([x1 * c - x2 * s, x2 * c + x1 * s], axis=-1).astype(x.dtype)


def pool_mixer(u, w_pool, pool_scale):
    B, S, _ = u.shape
    uf = u.astype(jnp.float32)
    cs = jnp.cumsum(uf, axis=1)
    cs = jnp.concatenate([jnp.zeros_like(cs[:, :1]), cs], axis=1)
    csg = cs.reshape(B, S + 1, POOL_GROUPS, POOL_GC)
    ug = uf.reshape(B, S, POOL_GROUPS, POOL_GC)
    t = jnp.arange(S)
    outs = []
    for g, w in enumerate(POOL_WINDOWS):
        lo = jnp.maximum(t + 1 - w, 0)
        win_sum = csg[:, 1:, g] - csg[:, lo, g]
        cnt = jnp.minimum(t + 1, w).astype(jnp.float32)[None, :, None]
        outs.append(win_sum / cnt - ug[:, :, g])
    pooled = jnp.stack(outs, axis=2).astype(u.dtype)
    y = jnp.einsum('bsgc,gcd->bsgd', pooled, w_pool) * pool_scale.reshape(POOL_GROUPS, POOL_GC)
    return y.reshape(B, S, POOL_W)


def conv_mixer(u, conv_dw, conv_dw_b, conv_ln_g, conv_ln_b, conv_pw, conv_pw_b):
    a, gate = jnp.split(u, 2, axis=-1)
    v = a * jax.nn.sigmoid(gate)
    v = lax.conv_general_dilated(
        v, conv_dw[:, None, :], window_strides=(1,), padding=[(CONV_K - 1, 0)],
        dimension_numbers=('NWC', 'WIO', 'NWC'), feature_group_count=CONV_W) + conv_dw_b
    v = jax.nn.silu(layernorm(v, conv_ln_g, conv_ln_b))
    return v @ conv_pw + conv_pw_b


def chunk_causal_attention(q, k, v):
    B, S, H, Dh = q.shape
    nb = S // Q_BLOCK
    scale = 1.0 / math.sqrt(Dh)
    qb = q.reshape(B, nb, Q_BLOCK, H, Dh).transpose(1, 0, 2, 3, 4)
    k_chunk = jnp.arange(S) // CHUNK

    def one_block(args):
        qi, i = args
        s = jnp.einsum('bqhd,bkhd->bhqk', qi, k).astype(jnp.float32) * scale
        q_chunk = (i * Q_BLOCK + jnp.arange(Q_BLOCK)) // CHUNK
        mask = k_chunk[None, :] <= q_chunk[:, None]
        s = jnp.where(mask, s, -jnp.inf)
        p = jax.nn.softmax(s, axis=-1).astype(v.dtype)
        return jnp.einsum('bhqk,bkhd->bqhd', p, v)

    o = lax.map(one_block, (qb, jnp.arange(nb)))
    return o.transpose(1, 0, 2, 3, 4).reshape(B, S, H, v.shape[-1])


def mla_mixer(c_q, c_kv, k_rope, cos, sin, q_a_g, w_uq, kv_a_g, w_ukv, q_norm_g, k_norm_g):
    B, S, _ = c_q.shape
    q = (rmsnorm(c_q, q_a_g) @ w_uq).reshape(B, S, N_HEADS, QK_HEAD)
    kv = (rmsnorm(c_kv, kv_a_g) @ w_ukv).reshape(B, S, N_HEADS, QK_NOPE + V_DIM)
    k_nope, v = kv[..., :QK_NOPE], kv[..., QK_NOPE:]
    k = jnp.concatenate([k_nope, jnp.broadcast_to(k_rope[:, :, None, :], (B, S, N_HEADS, QK_ROPE))], axis=-1)
    q = rmsnorm(q, q_norm_g)
    k = rmsnorm(k, k_norm_g)
    q = jnp.concatenate([q[..., :QK_NOPE], apply_rope(q[..., QK_NOPE:], cos, sin)], axis=-1)
    k = jnp.concatenate([k[..., :QK_NOPE], apply_rope(k[..., QK_NOPE:], cos, sin)], axis=-1)
    return chunk_causal_attention(q, k, v).reshape(B, S, ATTN_W)


def moe_ffn(h, w_router, b_router, w_gu, b_gu, w_down, b_down):
    B, S, D = h.shape
    T = B * S
    ht = h.reshape(T, D)
    logits = (ht @ w_router).astype(jnp.float32) + b_router.astype(jnp.float32)
    top_vals, top_idx = lax.top_k(logits, TOP_K)
    gates = jax.nn.softmax(top_vals, axis=-1)
    flat_e = top_idx.reshape(-1)
    flat_tok = jnp.repeat(jnp.arange(T, dtype=jnp.int32), TOP_K)
    flat_g = gates.reshape(-1)
    order = jnp.argsort(flat_e)
    se = flat_e[order]
    counts = jnp.bincount(flat_e, length=N_EXPERTS)
    padded = (counts + EXPERT_BLOCK - 1) // EXPERT_BLOCK * EXPERT_BLOCK
    start = jnp.cumsum(counts) - counts
    pend = jnp.cumsum(padded)
    pstart = pend - padded
    dest = pstart[se] + (jnp.arange(T * TOP_K) - start[se])
    n_blocks = -(-(T * TOP_K) // EXPERT_BLOCK) + N_EXPERTS
    n_rows = n_blocks * EXPERT_BLOCK
    row_tok = jnp.full((n_rows,), T, dtype=jnp.int32).at[dest].set(flat_tok[order])
    row_gate = jnp.zeros((n_rows,), jnp.float32).at[dest].set(flat_g[order])
    block_exp = jnp.minimum(jnp.searchsorted(pend, jnp.arange(n_blocks) * EXPERT_BLOCK, side='right'),
                            N_EXPERTS - 1)
    x_pad = jnp.concatenate([ht, jnp.zeros((1, D), ht.dtype)], axis=0)

    def expert_block(args):
        tok_b, e = args
        xb = x_pad[tok_b]
        hgu = xb @ w_gu[e] + b_gu[e]
        x_glu = jnp.minimum(hgu[:, ::2], SWIGLU_LIMIT)
        x_lin = jnp.clip(hgu[:, 1::2], -SWIGLU_LIMIT, SWIGLU_LIMIT)
        act = x_glu * jax.nn.sigmoid(SWIGLU_ALPHA * x_glu) * (x_lin + 1.0)
        return act @ w_down[e] + b_down[e]

    yb = lax.map(expert_block, (row_tok.reshape(n_blocks, EXPERT_BLOCK), block_exp))
    yb = yb.reshape(n_rows, D) * row_gate[:, None].astype(yb.dtype)
    out = jnp.zeros((T + 1, D), yb.dtype).at[row_tok].add(yb)
    return out[:T].reshape(B, S, D)


def setup_inputs(seed: int = 0) -> dict:
    key = jax.random.key(seed)
    ks = jax.random.split(key, 32)
    L = DEPTH
    f32 = jnp.float32

    def nrm(k, shape, fan_in):
        return jax.random.normal(k, shape, f32) * (fan_in ** -0.5)

    def gain(k, shape):
        return 1.0 + 0.05 * jax.random.normal(k, shape, f32)

    def bias(k, shape, s=0.02):
        return s * jax.random.normal(k, shape, f32)

    x = jax.random.normal(ks[0], (BATCH, SEQ, D_MODEL), f32)
    offset = jax.random.randint(ks[1], (BATCH, 1), 0, 100000, dtype=jnp.int32)
    positions = offset + jnp.arange(SEQ, dtype=jnp.int32)[None, :]
    return {
        "x": x,
        "positions": positions,
        "attn_norm_g": gain(ks[2], (L, D_MODEL)),
        "w_in": nrm(ks[3], (L, D_MODEL, D_IN), D_MODEL),
        "pool_w": nrm(ks[4], (L, POOL_GROUPS, POOL_GC, POOL_GC), POOL_GC),
        "pool_scale": gain(ks[5], (L, POOL_W)),
        "conv_dw": nrm(ks[6], (L, CONV_K, CONV_W), CONV_K),
        "conv_dw_b": bias(ks[7], (L, CONV_W)),
        "conv_ln_g": gain(ks[8], (L, CONV_W)),
        "conv_ln_b": bias(ks[9], (L, CONV_W)),
        "conv_pw": nrm(ks[10], (L, CONV_W, CONV_W), CONV_W),
        "conv_pw_b": bias(ks[11], (L, CONV_W)),
        "q_a_g": gain(ks[12], (L, Q_LORA)),
        "w_uq": nrm(ks[13], (L, Q_LORA, N_HEADS * QK_HEAD), Q_LORA),
        "kv_a_g": gain(ks[14], (L, KV_LORA)),
        "w_ukv": nrm(ks[15], (L, KV_LORA, N_HEADS * (QK_NOPE + V_DIM)), KV_LORA),
        "q_norm_g": gain(ks[16], (L, QK_HEAD)),
        "k_norm_g": gain(ks[17], (L, QK_HEAD)),
        "pool_out_g": gain(ks[18], (L, POOL_W)),
        "conv_out_g": gain(ks[19], (L, CONV_W)),
        "attn_out_g": gain(ks[20], (L, ATTN_W)),
        "w_o": nrm(ks[21], (L, D_MIX, D_MODEL), D_MIX),
        "ffn_norm_g": gain(ks[22], (L, D_MODEL)),
        "w_router": nrm(ks[23], (L, D_MODEL, N_EXPERTS), D_MODEL),
        "b_router": bias(ks[24], (L, N_EXPERTS), 0.01),
        "w_gu": nrm(ks[25], (L, N_EXPERTS, D_MODEL, 2 * D_FF), D_MODEL),
        "b_gu": bias(ks[26], (L, N_EXPERTS, 2 * D_FF)),
        "w_down": nrm(ks[27], (L, N_EXPERTS, D_FF, D_MODEL), D_FF),
        "b_down": bias(ks[28], (L, N_EXPERTS, D_MODEL)),
    }


def reference(x, positions, attn_norm_g, w_in, pool_w, pool_scale, conv_dw, conv_dw_b, conv_ln_g,
              conv_ln_b, conv_pw, conv_pw_b, q_a_g, w_uq, kv_a_g, w_ukv, q_norm_g, k_norm_g,
              pool_out_g, conv_out_g, attn_out_g, w_o, ffn_norm_g, w_router, b_router, w_gu, b_gu,
              w_down, b_down):
    cos, sin = rope_tables(positions)
    s0 = POOL_W
    s1 = s0 + 2 * CONV_W
    s2 = s1 + Q_LORA
    s3 = s2 + KV_LORA
    for l in range(DEPTH):
        h = rmsnorm(x, attn_norm_g[l])
        z = h @ w_in[l]
        y_pool = pool_mixer(z[..., :s0], pool_w[l], pool_scale[l])
        y_conv = conv_mixer(z[..., s0:s1], conv_dw[l], conv_dw_b[l], conv_ln_g[l], conv_ln_b[l],
                            conv_pw[l], conv_pw_b[l])
        y_attn = mla_mixer(z[..., s1:s2], z[..., s2:s3], z[..., s3:], cos, sin, q_a_g[l], w_uq[l],
                           kv_a_g[l], w_ukv[l], q_norm_g[l], k_norm_g[l])
        mix = jnp.concatenate([rmsnorm(y_pool, pool_out_g[l]), rmsnorm(y_conv, conv_out_g[l]),
                               rmsnorm(y_attn, attn_out_g[l])], axis=-1)
        x = x + mix @ w_o[l]
        x = x + moe_ffn(rmsnorm(x, ffn_norm_g[l]), w_router[l], b_router[l], w_gu[l], b_gu[l],
                        w_down[l], b_down[l])
    return x
```

```python
import functools
import math

import jax
import jax.numpy as jnp
from jax import lax
from jax.experimental import pallas as pl
from jax.experimental.pallas import tpu as pltpu

D_MODEL = 1024
CHUNK = 64
POOL_W = 256
POOL_WINDOWS = (2, 4, 8, 16)
POOL_GC = 64
CONV_W = 256
CONV_K = 31
N_HEADS = 4
QK_NOPE = 128
QK_ROPE = 64
QK_HEAD = QK_NOPE + QK_ROPE
V_DIM = 128
Q_LORA = 256
KV_LORA = 128
ATTN_W = N_HEADS * V_DIM
ROPE_THETA = 10000.0
N_EXPERTS = 32
TOP_K = 4
D_FF = D_MODEL
SWIGLU_LIMIT = 7.0
SWIGLU_ALPHA = 1.702
EPS = 1e-6

LANES = 128
HEAD_PAD = 2 * LANES
D_IN_PAD = 1280
POOL_HALO = 16
CONV_HALO = 32
NEG_BIG = -1e30

TM_FRONT = 512
TQ = 512
TM_POST = 512
T_DISPATCH = 256
EXPERT_BLOCK = 256
FF_CHUNK = 512
VMEM_LIMIT = 56 * 1024 * 1024

_bf16 = jnp.bfloat16
_f32 = jnp.float32


def _dot(a, b):
    return jnp.dot(a, b, preferred_element_type=_f32)


def _rms(x, width):
    return lax.rsqrt(jnp.sum(x * x, axis=-1, keepdims=True) * (1.0 / width) + EPS)


def _layer_spec(shape, layer, n_grid):
    nd = len(shape)
    block = (None,) + tuple(shape[1:])
    if n_grid == 1:
        return pl.BlockSpec(block, lambda i: (layer,) + (0,) * (nd - 1))
    return pl.BlockSpec(block, lambda i, j: (layer,) + (0,) * (nd - 1))


def _rope(x, c_tab, s_tab):
    lane = lax.broadcasted_iota(jnp.int32, x.shape, 1)
    partner = jnp.where(lane < QK_ROPE // 2, pltpu.roll(x, LANES - QK_ROPE // 2, 1),
                        pltpu.roll(x, QK_ROPE // 2, 1))
    return x * c_tab + partner * s_tab


def _front_kernel(x_ref, ropec_ref, ropes_ref, ang_ref, win_ref, poolw_ref, pscale_ref, cdw_ref, cdwb_ref,
                  clng_ref, clnb_ref, cpw_ref, cpwb_ref, qag_ref, wuq_ref, kvag_ref, wukv_ref, qng_ref,
                  kng_ref, poutg_ref, coutg_ref,
                  mix_ref, q_ref, k_ref, v_ref, pool_ext, conv_ext):
    j = pl.program_id(1)
    tm = x_ref.shape[0]

    x = x_ref[...]
    h = x * _rms(x, D_MODEL) * ang_ref[...]
    z = _dot(h.astype(_bf16), win_ref[...])

    u = z[:, 0:POOL_W]

    @pl.when(j == 0)
    def _():
        pool_ext[0:POOL_HALO, :] = jnp.zeros((POOL_HALO, POOL_W), _f32)
        conv_ext[0:CONV_HALO, :] = jnp.zeros((CONV_HALO, CONV_W), _f32)

    @pl.when(j > 0)
    def _():
        pool_ext[0:POOL_HALO, :] = pool_ext[tm:tm + POOL_HALO, :]
        conv_ext[0:CONV_HALO, :] = conv_ext[tm:tm + CONV_HALO, :]

    pool_ext[POOL_HALO:POOL_HALO + tm, :] = u
    t_pos = (j * tm + lax.broadcasted_iota(jnp.int32, (tm, LANES), 0) + 1).astype(_f32)
    lane = lax.broadcasted_iota(jnp.int32, (tm, LANES), 1)
    pooled_halves = []
    for half, (w_lo, w_hi) in enumerate(((POOL_WINDOWS[0], POOL_WINDOWS[1]),
                                         (POOL_WINDOWS[2], POOL_WINDOWS[3]))):
        cols = slice(half * LANES, (half + 1) * LANES)
        s_lo = pool_ext[POOL_HALO:POOL_HALO + tm, cols]
        for d in range(1, w_lo):
            s_lo = s_lo + pool_ext[POOL_HALO - d:POOL_HALO - d + tm, cols]
        s_hi = s_lo
        for d in range(w_lo, w_hi):
            s_hi = s_hi + pool_ext[POOL_HALO - d:POOL_HALO - d + tm, cols]
        first = lane < POOL_GC
        win_sum = jnp.where(first, s_lo, s_hi)
        cnt = jnp.minimum(t_pos, jnp.where(first, float(w_lo), float(w_hi)))
        pooled_halves.append(win_sum / cnt - u[:, cols])
    pooled = jnp.concatenate(pooled_halves, axis=-1)
    y_pool = _dot(pooled.astype(_bf16), poolw_ref[...]) * pscale_ref[...]
    y_pool = y_pool * _rms(y_pool, POOL_W) * poutg_ref[...]

    a = z[:, POOL_W:POOL_W + CONV_W]
    gate = z[:, POOL_W + CONV_W:POOL_W + 2 * CONV_W]
    conv_ext[CONV_HALO:CONV_HALO + tm, :] = a * jax.nn.sigmoid(gate)
    base = CONV_HALO - (CONV_K - 1)
    acc = conv_ext[base:base + tm, :] * cdw_ref[0:1, :]
    for kk in range(1, CONV_K):
        acc = acc + conv_ext[base + kk:base + kk + tm, :] * cdw_ref[kk:kk + 1, :]
    acc = acc + cdwb_ref[...]
    mu = jnp.mean(acc, axis=-1, keepdims=True)
    cen = acc - mu
    var = jnp.mean(cen * cen, axis=-1, keepdims=True)
    ln = cen * lax.rsqrt(var + EPS) * clng_ref[...] + clnb_ref[...]
    sw = ln * jax.nn.sigmoid(ln)
    y_conv = _dot(sw.astype(_bf16), cpw_ref[...]) + cpwb_ref[...]
    y_conv = y_conv * _rms(y_conv, CONV_W) * coutg_ref[...]

    mix_ref[:, 0:POOL_W] = y_pool.astype(_bf16)
    mix_ref[:, POOL_W:POOL_W + CONV_W] = y_conv.astype(_bf16)

    s1 = POOL_W + 2 * CONV_W
    c_q = z[:, s1:s1 + Q_LORA]
    c_kv = z[:, s1 + Q_LORA:s1 + Q_LORA + KV_LORA]
    k_rope = z[:, s1 + Q_LORA + KV_LORA:s1 + Q_LORA + KV_LORA + LANES]
    c_tab = ropec_ref[...]
    s_tab = ropes_ref[...]

    q_all = _dot((c_q * _rms(c_q, Q_LORA) * qag_ref[...]).astype(_bf16), wuq_ref[...])
    kv_all = _dot((c_kv * _rms(c_kv, KV_LORA) * kvag_ref[...]).astype(_bf16), wukv_ref[...])
    sm_scale = 1.0 / math.sqrt(QK_HEAD)
    kr_ss = jnp.sum(k_rope * k_rope, axis=-1, keepdims=True)
    kr_rot = _rope(k_rope * kng_ref[:, LANES:2 * LANES], c_tab, s_tab)
    for hd in range(N_HEADS):
        qh = q_all[:, hd * HEAD_PAD:(hd + 1) * HEAD_PAD]
        qh = qh * (_rms(qh, QK_HEAD) * sm_scale) * qng_ref[...]
        q_ref[:, hd * HEAD_PAD:hd * HEAD_PAD + LANES] = qh[:, 0:LANES].astype(_bf16)
        q_ref[:, hd * HEAD_PAD + LANES:(hd + 1) * HEAD_PAD] = _rope(qh[:, LANES:], c_tab, s_tab).astype(_bf16)
        kn = kv_all[:, hd * QK_NOPE:(hd + 1) * QK_NOPE]
        k_rs = lax.rsqrt((jnp.sum(kn * kn, axis=-1, keepdims=True) + kr_ss) * (1.0 / QK_HEAD) + EPS)
        k_ref[:, hd * HEAD_PAD:hd * HEAD_PAD + LANES] = (kn * k_rs * kng_ref[:, 0:LANES]).astype(_bf16)
        k_ref[:, hd * HEAD_PAD + LANES:(hd + 1) * HEAD_PAD] = (kr_rot * k_rs).astype(_bf16)
    v_ref[...] = kv_all[:, N_HEADS * QK_NOPE:].astype(_bf16)


def _mixer_front(layer, x, rope_c, rope_s, p):
    B, S, _ = x.shape
    tm = TM_FRONT
    row = lambda w: pl.BlockSpec((None, tm, w), lambda b, j: (b, j, 0))
    names = ("attn_norm_g", "w_in", "pool_w", "pool_scale", "conv_dw", "conv_dw_b", "conv_ln_g", "conv_ln_b",
             "conv_pw", "conv_pw_b", "q_a_g", "w_uq", "kv_a_g", "w_ukv", "q_norm_g", "k_norm_g",
             "pool_out_g", "conv_out_g")
    weights = [p[n] for n in names]
    in_specs = [row(D_MODEL), row(LANES), row(LANES)] + [_layer_spec(w.shape, layer, 2) for w in weights]
    out_shape = (jax.ShapeDtypeStruct((B, S, POOL_W + CONV_W), _bf16),
                 jax.ShapeDtypeStruct((B, S, N_HEADS * HEAD_PAD), _bf16),
                 jax.ShapeDtypeStruct((B, S, N_HEADS * HEAD_PAD), _bf16),
                 jax.ShapeDtypeStruct((B, S, ATTN_W), _bf16))
    out_specs = (row(POOL_W + CONV_W), row(N_HEADS * HEAD_PAD), row(N_HEADS * HEAD_PAD), row(ATTN_W))
    return pl.pallas_call(
        _front_kernel,
        grid=(B, S // tm),
        in_specs=in_specs,
        out_specs=out_specs,
        out_shape=out_shape,
        scratch_shapes=[pltpu.VMEM((tm + POOL_HALO, POOL_W), _f32),
                        pltpu.VMEM((tm + CONV_HALO, CONV_W), _f32)],
        compiler_params=pltpu.CompilerParams(dimension_semantics=("arbitrary", "arbitrary"),
                                             vmem_limit_bytes=VMEM_LIMIT),
        name="mixer_front",
    )(x, rope_c, rope_s, *weights)


def _attn_kernel(qi_tab, ki_tab, q_ref, k_ref, v_ref, o_ref, m_scr, l_scr, acc_scr):
    step = pl.program_id(2)
    qi = qi_tab[step]
    ki = ki_tab[step]

    @pl.when(ki == 0)
    def _():
        m_scr[...] = jnp.full(m_scr.shape, NEG_BIG, _f32)
        l_scr[...] = jnp.zeros(l_scr.shape, _f32)
        acc_scr[...] = jnp.zeros(acc_scr.shape, _f32)

    s = lax.dot_general(q_ref[...], k_ref[...], (((1,), (1,)), ((), ())), preferred_element_type=_f32)
    tq, tk = s.shape
    q_chunk = (qi * tq + lax.broadcasted_iota(jnp.int32, s.shape, 0)) // CHUNK
    k_chunk = (ki * tk + lax.broadcasted_iota(jnp.int32, s.shape, 1)) // CHUNK
    s = jnp.where(k_chunk <= q_chunk, s, NEG_BIG)
    m_prev = m_scr[...]
    m_new = jnp.maximum(m_prev, jnp.max(s, axis=-1, keepdims=True))
    alpha = jnp.exp(m_prev - m_new)
    p = jnp.exp(s - m_new)
    l_scr[...] = alpha * l_scr[...] + jnp.sum(p, axis=-1, keepdims=True)
    acc_scr[...] = alpha * acc_scr[...] + _dot(p.astype(_bf16), v_ref[...])
    m_scr[...] = m_new

    @pl.when(ki == qi)
    def _():
        o_ref[...] = (acc_scr[...] / l_scr[...]).astype(o_ref.dtype)


def _attention(q, k, v):
    B, S, _ = q.shape
    nq = S // TQ
    pairs = [(a, b) for a in range(nq) for b in range(a + 1)]
    qi_tab = jnp.asarray([a for a, _ in pairs], jnp.int32)
    ki_tab = jnp.asarray([b for _, b in pairs], jnp.int32)
    grid_spec = pltpu.PrefetchScalarGridSpec(
        num_scalar_prefetch=2,
        grid=(B, N_HEADS, len(pairs)),
        in_specs=[pl.BlockSpec((None, TQ, HEAD_PAD), lambda b, h, s, qt, kt: (b, qt[s], h)),
                  pl.BlockSpec((None, TQ, HEAD_PAD), lambda b, h, s, qt, kt: (b, kt[s], h)),
                  pl.BlockSpec((None, TQ, V_DIM), lambda b, h, s, qt, kt: (b, kt[s], h))],
        out_specs=pl.BlockSpec((None, TQ, V_DIM), lambda b, h, s, qt, kt: (b, qt[s], h)),
        scratch_shapes=[pltpu.VMEM((TQ, 1), _f32), pltpu.VMEM((TQ, 1), _f32), pltpu.VMEM((TQ, V_DIM), _f32)],
    )
    return pl.pallas_call(
        _attn_kernel,
        grid_spec=grid_spec,
        out_shape=jax.ShapeDtypeStruct((B, S, ATTN_W), _bf16),
        compiler_params=pltpu.CompilerParams(dimension_semantics=("arbitrary", "arbitrary", "arbitrary"),
                                             vmem_limit_bytes=VMEM_LIMIT),
        name="attention",
    )(qi_tab, ki_tab, q, k, v)


def _post_kernel(x_ref, mix_ref, att_ref, aog_ref, wo_ref, fng_ref, wr_ref, br_ref,
                 x1_ref, h2_ref, ridx_ref, rgate_ref, cnt_ref, carry):
    i = pl.program_id(0)
    tm = x_ref.shape[0]

    @pl.when(i == 0)
    def _():
        carry[...] = jnp.zeros(carry.shape, _f32)

    att = att_ref[...].astype(_f32)
    att_n = (att * _rms(att, ATTN_W) * aog_ref[...]).astype(_bf16)
    x1 = x_ref[...] + _dot(mix_ref[...], wo_ref[0:POOL_W + CONV_W, :]) + _dot(att_n, wo_ref[POOL_W + CONV_W:, :])
    x1_ref[...] = x1
    h2 = x1 * _rms(x1, D_MODEL) * fng_ref[...]
    h2_ref[...] = h2

    logits = _dot(h2.astype(_bf16), wr_ref[...]) + br_ref[...]
    lane = lax.broadcasted_iota(jnp.int32, logits.shape, 1)
    lane_f = lane.astype(_f32)
    work = logits
    sel = jnp.zeros(logits.shape, _f32)
    onehots, vals, idxs = [], [], []
    for _ in range(TOP_K):
        mx = jnp.max(work, axis=-1, keepdims=True)
        idx_f = jnp.min(jnp.where(work == mx, lane_f, float(LANES)), axis=-1, keepdims=True)
        idx = idx_f.astype(jnp.int32)
        hot = lane == idx
        onehots.append(hot)
        vals.append(mx)
        idxs.append(idx)
        sel = jnp.where(hot, 1.0, sel)
        work = jnp.where(hot, -jnp.inf, work)
    exps = [jnp.exp(vk - vals[0]) for vk in vals]
    denom = exps[0] + exps[1] + exps[2] + exps[3]

    r_i = lax.broadcasted_iota(jnp.int32, (tm, tm), 0)
    c_i = lax.broadcasted_iota(jnp.int32, (tm, tm), 1)
    tri = jnp.where(c_i < r_i, 1.0, 0.0).astype(_bf16)
    rank = _dot(tri, sel.astype(_bf16)) + carry[0:1, :]
    new_carry = carry[0:1, :] + jnp.sum(sel, axis=0, keepdims=True)
    carry[...] = jnp.broadcast_to(new_carry, carry.shape)
    cnt_ref[...] = jnp.broadcast_to(new_carry, cnt_ref.shape)

    ridx = jnp.zeros(logits.shape, jnp.int32)
    rgate = jnp.zeros(logits.shape, _f32)
    for kk in range(TOP_K):
        pos = jnp.sum(jnp.where(onehots[kk], rank, 0.0), axis=-1, keepdims=True).astype(jnp.int32)
        ridx = jnp.where(lane == kk, idxs[kk], ridx)
        ridx = jnp.where(lane == TOP_K + kk, pos, ridx)
        rgate = jnp.where(lane == kk, exps[kk] / denom, rgate)
    ridx_ref[...] = ridx
    rgate_ref[...] = rgate


def _post_mix(layer, x, mix, att, p):
    T = x.shape[0]
    tm = TM_POST
    row = lambda w: pl.BlockSpec((tm, w), lambda i: (i, 0))
    names = ("attn_out_g", "w_o", "ffn_norm_g", "w_router", "b_router")
    weights = [p[n] for n in names]
    out_shape = (jax.ShapeDtypeStruct((T, D_MODEL), _f32), jax.ShapeDtypeStruct((T, D_MODEL), _f32),
                 jax.ShapeDtypeStruct((T, LANES), jnp.int32), jax.ShapeDtypeStruct((T, LANES), _f32),
                 jax.ShapeDtypeStruct((8, LANES), _f32))
    return pl.pallas_call(
        _post_kernel,
        grid=(T // tm,),
        in_specs=[row(D_MODEL), row(POOL_W + CONV_W), row(ATTN_W)] + [_layer_spec(w.shape, layer, 1) for w in weights],
        out_specs=(row(D_MODEL), row(D_MODEL), row(LANES), row(LANES), pl.BlockSpec((8, LANES), lambda i: (0, 0))),
        out_shape=out_shape,
        scratch_shapes=[pltpu.VMEM((8, LANES), _f32)],
        compiler_params=pltpu.CompilerParams(dimension_semantics=("arbitrary",), vmem_limit_bytes=VMEM_LIMIT),
        name="post_mix",
    )(x, mix, att, *weights)


def _dispatch_kernel(pad_start, pad_count, dest_ref, h2_ref, xs_hbm, zero_row, sem):
    i = pl.program_id(0)
    td = h2_ref.shape[0]

    def row_copy(src_row_ref, dst_row):
        return pltpu.make_async_copy(src_row_ref, xs_hbm.at[pl.ds(dst_row, 1)], sem)

    @pl.when(i == 0)
    def _():
        zero_row[...] = jnp.zeros(zero_row.shape, _f32)
        for e in range(N_EXPERTS):
            def start_pad(r, c, e=e):
                row_copy(zero_row, pad_start[e] + r).start()
                return c
            lax.fori_loop(0, pad_count[e], start_pad, 0)
        for e in range(N_EXPERTS):
            def wait_pad(r, c, e=e):
                row_copy(zero_row, pad_start[e] + r).wait()
                return c
            lax.fori_loop(0, pad_count[e], wait_pad, 0)

    def start_tok(t, c):
        for kk in range(TOP_K):
            row_copy(h2_ref.at[pl.ds(t, 1)], dest_ref[0, 0, t * TOP_K + kk]).start()
        return c

    def wait_tok(t, c):
        for kk in range(TOP_K):
            row_copy(h2_ref.at[pl.ds(t, 1)], dest_ref[0, 0, t * TOP_K + kk]).wait()
        return c

    lax.fori_loop(0, td, start_tok, 0)
    lax.fori_loop(0, td, wait_tok, 0)


def _dispatch(h2, dest, pad_start, pad_count, n_rows):
    T = h2.shape[0]
    td = T_DISPATCH
    dest3 = dest.reshape(T // td, 1, td * TOP_K)
    grid_spec = pltpu.PrefetchScalarGridSpec(
        num_scalar_prefetch=2,
        grid=(T // td,),
        in_specs=[pl.BlockSpec((1, 1, td * TOP_K), lambda i, ps, pc: (i, 0, 0), memory_space=pltpu.SMEM),
                  pl.BlockSpec((td, D_MODEL), lambda i, ps, pc: (i, 0))],
        out_specs=pl.BlockSpec(memory_space=pl.ANY),
        scratch_shapes=[pltpu.VMEM((1, D_MODEL), _f32), pltpu.SemaphoreType.DMA(())],
    )
    return pl.pallas_call(
        _dispatch_kernel,
        grid_spec=grid_spec,
        out_shape=jax.ShapeDtypeStruct((n_rows, D_MODEL), _f32),
        compiler_params=pltpu.CompilerParams(dimension_semantics=("arbitrary",), vmem_limit_bytes=VMEM_LIMIT),
        name="dispatch",
    )(pad_start, pad_count, dest3, h2)


def _expert_kernel(blk_exp, blk_valid, xs_ref, wgu_ref, bgu_ref, wd_ref, bd_ref, y_ref):
    i = pl.program_id(0)
    valid = blk_valid[i]

    @pl.when(valid > 0)
    def _():
        x = xs_ref[...].astype(_bf16)
        y = jnp.zeros(y_ref.shape, _f32)
        for c in range(D_FF // FF_CHUNK):
            lo = c * FF_CHUNK
            hg = _dot(x, wgu_ref[:, lo:lo + FF_CHUNK]) + bgu_ref[:, lo:lo + FF_CHUNK]
            hl = _dot(x, wgu_ref[:, D_FF + lo:D_FF + lo + FF_CHUNK]) + bgu_ref[:, D_FF + lo:D_FF + lo + FF_CHUNK]
            g = jnp.minimum(hg, SWIGLU_LIMIT)
            lin = jnp.clip(hl, -SWIGLU_LIMIT, SWIGLU_LIMIT)
            act = g * jax.nn.sigmoid(SWIGLU_ALPHA * g) * (lin + 1.0)
            y = y + _dot(act.astype(_bf16), wd_ref[lo:lo + FF_CHUNK, :])
        y_ref[...] = y + bd_ref[...]

    @pl.when(valid == 0)
    def _():
        y_ref[...] = jnp.zeros(y_ref.shape, _f32)


def _experts(layer, xs, blk_exp, blk_valid, w_gu, b_gu, w_down, b_down):
    n_rows = xs.shape[0]
    nb = n_rows // EXPERT_BLOCK
    xmap = lambda i, be, bv: (jnp.where(bv[i] > 0, i, 0), 0)
    grid_spec = pltpu.PrefetchScalarGridSpec(
        num_scalar_prefetch=2,
        grid=(nb,),
        in_specs=[pl.BlockSpec((EXPERT_BLOCK, D_MODEL), xmap),
                  pl.BlockSpec((None, None, D_MODEL, 2 * D_FF), lambda i, be, bv: (layer, be[i], 0, 0)),
                  pl.BlockSpec((None, None, 1, 2 * D_FF), lambda i, be, bv: (layer, be[i], 0, 0)),
                  pl.BlockSpec((None, None, D_FF, D_MODEL), lambda i, be, bv: (layer, be[i], 0, 0)),
                  pl.BlockSpec((None, None, 1, D_MODEL), lambda i, be, bv: (layer, be[i], 0, 0))],
        out_specs=pl.BlockSpec((EXPERT_BLOCK, D_MODEL), lambda i, be, bv: (i, 0)),
    )
    return pl.pallas_call(
        _expert_kernel,
        grid_spec=grid_spec,
        out_shape=jax.ShapeDtypeStruct((n_rows, D_MODEL), _f32),
        compiler_params=pltpu.CompilerParams(dimension_semantics=("arbitrary",), vmem_limit_bytes=VMEM_LIMIT),
        name="experts",
    )(blk_exp, blk_valid, xs, w_gu, b_gu, w_down, b_down)


def _combine_kernel(dest_ref, x1_ref, gate_ref, ys_hbm, out_ref, buf, sem):
    tc = x1_ref.shape[0]

    def row_copy(t, kk):
        return pltpu.make_async_copy(ys_hbm.at[pl.ds(dest_ref[0, 0, t * TOP_K + kk], 1)],
                                     buf.at[kk, pl.ds(t, 1)], sem)

    def start_tok(t, c):
        for kk in range(TOP_K):
            row_copy(t, kk).start()
        return c

    def wait_tok(t, c):
        for kk in range(TOP_K):
            row_copy(t, kk).wait()
        return c

    lax.fori_loop(0, tc, start_tok, 0)
    lax.fori_loop(0, tc, wait_tok, 0)
    gates = gate_ref[...]
    out = x1_ref[...]
    for kk in range(TOP_K):
        out = out + gates[:, kk:kk + 1] * buf[kk]
    out_ref[...] = out


def _combine(x1, rgate, dest, ys):
    T = x1.shape[0]
    tc = T_DISPATCH
    dest3 = dest.reshape(T // tc, 1, tc * TOP_K)
    return pl.pallas_call(
        _combine_kernel,
        grid=(T // tc,),
        in_specs=[pl.BlockSpec((1, 1, tc * TOP_K), lambda i: (i, 0, 0), memory_space=pltpu.SMEM),
                  pl.BlockSpec((tc, D_MODEL), lambda i: (i, 0)),
                  pl.BlockSpec((tc, LANES), lambda i: (i, 0)),
                  pl.BlockSpec(memory_space=pl.ANY)],
        out_specs=pl.BlockSpec((tc, D_MODEL), lambda i: (i, 0)),
        out_shape=jax.ShapeDtypeStruct((T, D_MODEL), _f32),
        scratch_shapes=[pltpu.VMEM((TOP_K, tc, D_MODEL), _f32), pltpu.SemaphoreType.DMA(())],
        compiler_params=pltpu.CompilerParams(dimension_semantics=("arbitrary",), vmem_limit_bytes=VMEM_LIMIT),
        name="combine",
    )(dest3, x1, rgate, ys)


def _prepare_params(w):
    L = w["w_in"].shape[0]
    p = {}
    row = lambda a: a.reshape(L, 1, a.shape[-1])
    p["attn_norm_g"] = row(w["attn_norm_g"])
    p["w_in"] = jnp.pad(w["w_in"], ((0, 0), (0, 0), (0, D_IN_PAD - w["w_in"].shape[-1]))).astype(_bf16)
    eye = jnp.eye(len(POOL_WINDOWS), dtype=_f32)
    p["pool_w"] = jnp.einsum("lgcd,gh->lgchd", w["pool_w"], eye).reshape(L, POOL_W, POOL_W).astype(_bf16)
    p["pool_scale"] = row(w["pool_scale"])
    p["conv_dw"] = w["conv_dw"]
    p["conv_dw_b"] = row(w["conv_dw_b"])
    p["conv_ln_g"] = row(w["conv_ln_g"])
    p["conv_ln_b"] = row(w["conv_ln_b"])
    p["conv_pw"] = w["conv_pw"].astype(_bf16)
    p["conv_pw_b"] = row(w["conv_pw_b"])
    p["q_a_g"] = row(w["q_a_g"])
    wuq = w["w_uq"].reshape(L, Q_LORA, N_HEADS, QK_HEAD)
    p["w_uq"] = jnp.pad(wuq, ((0, 0), (0, 0), (0, 0), (0, HEAD_PAD - QK_HEAD))).reshape(
        L, Q_LORA, N_HEADS * HEAD_PAD).astype(_bf16)
    p["kv_a_g"] = row(w["kv_a_g"])
    wukv = w["w_ukv"].reshape(L, KV_LORA, N_HEADS, QK_NOPE + V_DIM)
    p["w_ukv"] = jnp.concatenate([wukv[..., :QK_NOPE].reshape(L, KV_LORA, N_HEADS * QK_NOPE),
                                  wukv[..., QK_NOPE:].reshape(L, KV_LORA, N_HEADS * V_DIM)], axis=-1).astype(_bf16)
    pad_head = lambda g: jnp.pad(g, ((0, 0), (0, HEAD_PAD - QK_HEAD))).reshape(L, 1, HEAD_PAD)
    p["q_norm_g"] = pad_head(w["q_norm_g"])
    p["k_norm_g"] = pad_head(w["k_norm_g"])
    p["pool_out_g"] = row(w["pool_out_g"])
    p["conv_out_g"] = row(w["conv_out_g"])
    p["attn_out_g"] = row(w["attn_out_g"])
    p["w_o"] = w["w_o"].astype(_bf16)
    p["ffn_norm_g"] = row(w["ffn_norm_g"])
    p["w_router"] = jnp.pad(w["w_router"], ((0, 0), (0, 0), (0, LANES - N_EXPERTS))).astype(_bf16)
    p["b_router"] = jnp.pad(w["b_router"], ((0, 0), (0, LANES - N_EXPERTS)),
                            constant_values=NEG_BIG).reshape(L, 1, LANES)
    E = w["w_gu"].shape[1]
    wgu = w["w_gu"].reshape(L, E, D_MODEL, D_FF, 2)
    p["w_gu"] = jnp.concatenate([wgu[..., 0], wgu[..., 1]], axis=-1).astype(_bf16)
    bgu = w["b_gu"].reshape(L, E, D_FF, 2)
    p["b_gu"] = jnp.concatenate([bgu[..., 0], bgu[..., 1]], axis=-1).reshape(L, E, 1, 2 * D_FF)
    p["w_down"] = w["w_down"].astype(_bf16)
    p["b_down"] = w["b_down"].reshape(L, E, 1, D_MODEL)
    return p


def _rope_tables(positions):
    inv_freq = 1.0 / (ROPE_THETA ** (jnp.arange(0, QK_ROPE, 2, dtype=_f32) / QK_ROPE))
    ang = positions.astype(_f32)[..., None] * inv_freq
    cos, sin = jnp.cos(ang), jnp.sin(ang)
    zeros = jnp.zeros(cos.shape[:-1] + (LANES - QK_ROPE,), _f32)
    return (jnp.concatenate([cos, cos, zeros], axis=-1), jnp.concatenate([-sin, sin, zeros], axis=-1))


def _routing_layout(ridx, counts_f):
    T = ridx.shape[0]
    experts = ridx[:, 0:TOP_K]
    pos = ridx[:, TOP_K:2 * TOP_K]
    counts = counts_f[0, 0:N_EXPERTS].astype(jnp.int32)
    padded = (counts + EXPERT_BLOCK - 1) // EXPERT_BLOCK * EXPERT_BLOCK
    pend = jnp.cumsum(padded)
    pstart = pend - padded
    dest = pstart[experts] + pos
    n_blocks = -(-(T * TOP_K) // EXPERT_BLOCK) + N_EXPERTS
    blk_row = jnp.arange(n_blocks, dtype=jnp.int32) * EXPERT_BLOCK
    blk_exp = jnp.minimum(jnp.searchsorted(pend, blk_row, side="right"), N_EXPERTS - 1).astype(jnp.int32)
    blk_valid = jnp.clip(pstart[blk_exp] + counts[blk_exp] - blk_row, 0, EXPERT_BLOCK)
    blk_valid = jnp.where(blk_row < pend[-1], blk_valid, 0).astype(jnp.int32)
    return dest.astype(jnp.int32), blk_exp, blk_valid, (pstart + counts).astype(jnp.int32), \
        (padded - counts).astype(jnp.int32), n_blocks * EXPERT_BLOCK


def kernel(x, positions, attn_norm_g, w_in, pool_w, pool_scale, conv_dw, conv_dw_b, conv_ln_g, conv_ln_b, conv_pw, conv_pw_b, q_a_g, w_uq, kv_a_g, w_ukv, q_norm_g, k_norm_g, pool_out_g, conv_out_g, attn_out_g, w_o, ffn_norm_g, w_router, b_router, w_gu, b_gu, w_down, b_down):
    B, S, D = x.shape
    T = B * S
    depth = w_in.shape[0]
    p = _prepare_params(dict(
        attn_norm_g=attn_norm_g, w_in=w_in, pool_w=pool_w, pool_scale=pool_scale, conv_dw=conv_dw,
        conv_dw_b=conv_dw_b, conv_ln_g=conv_ln_g, conv_ln_b=conv_ln_b, conv_pw=conv_pw, conv_pw_b=conv_pw_b,
        q_a_g=q_a_g, w_uq=w_uq, kv_a_g=kv_a_g, w_ukv=w_ukv, q_norm_g=q_norm_g, k_norm_g=k_norm_g,
        pool_out_g=pool_out_g, conv_out_g=conv_out_g, attn_out_g=attn_out_g, w_o=w_o, ffn_norm_g=ffn_norm_g,
        w_router=w_router, b_router=b_router, w_gu=w_gu, b_gu=b_gu, w_down=w_down, b_down=b_down))
    rope_c, rope_s = _rope_tables(positions)
    for layer in range(depth):
        mix, q, k, v = _mixer_front(layer, x, rope_c, rope_s, p)
        att = _attention(q, k, v)
        x1, h2, ridx, rgate, counts = _post_mix(layer, x.reshape(T, D), mix.reshape(T, -1), att.reshape(T, -1), p)
        dest, blk_exp, blk_valid, pad_start, pad_count, n_rows = _routing_layout(ridx, counts)
        xs = _dispatch(h2, dest, pad_start, pad_count, n_rows)
        ys = _experts(layer, xs, blk_exp, blk_valid, p["w_gu"], p["b_gu"], p["w_down"], p["b_down"])
        x = _combine(x1, rgate, dest, ys).reshape(B, S, D)
    return x
```

```python
import functools
import math

import jax
import jax.numpy as jnp
from jax import lax
from jax.experimental import pallas as pl
from jax.experimental.pallas import tpu as pltpu

D_MODEL = 1024
CHUNK = 64
POOL_W = 256
POOL_WINDOWS = (2, 4, 8, 16)
POOL_GC = 64
CONV_W = 256
CONV_K = 31
N_HEADS = 4
QK_NOPE = 128
QK_ROPE = 64
QK_HEAD = QK_NOPE + QK_ROPE
V_DIM = 128
Q_LORA = 256
KV_LORA = 128
ATTN_W = N_HEADS * V_DIM
ROPE_THETA = 10000.0
N_EXPERTS = 32
TOP_K = 4
D_FF = D_MODEL
SWIGLU_LIMIT = 7.0
SWIGLU_ALPHA = 1.702
EPS = 1e-6

LANES = 128
HEAD_PAD = 2 * LANES
D_IN_PAD = 1280
POOL_HALO = 16
CONV_HALO = 32
NEG_BIG = -1e30

TM_FRONT = 512
TQ = 512
TM_POST = 1024
ROUTE_TILE = 128
ROW_CHUNK = 8
SUBLANES = 8
STAGE_ROWS = -(-(ROUTE_TILE * TOP_K + N_EXPERTS * (ROW_CHUNK - 1)) // LANES) * LANES
EXPERT_BLOCK = 256
FF_CHUNK = 512
VMEM_LIMIT = 56 * 1024 * 1024

_bf16 = jnp.bfloat16
_f32 = jnp.float32


def _dot(a, b):
    return jnp.dot(a, b, preferred_element_type=_f32)


def _rms(x, width):
    return lax.rsqrt(jnp.sum(x * x, axis=-1, keepdims=True) * (1.0 / width) + EPS)


def _layer_spec(shape, layer, n_grid):
    nd = len(shape)
    block = (None,) + tuple(shape[1:])
    if n_grid == 1:
        return pl.BlockSpec(block, lambda i: (layer,) + (0,) * (nd - 1))
    return pl.BlockSpec(block, lambda i, j: (layer,) + (0,) * (nd - 1))


def _rope(x, c_tab, s_tab):
    lane = lax.broadcasted_iota(jnp.int32, x.shape, 1)
    partner = jnp.where(lane < QK_ROPE // 2, pltpu.roll(x, LANES - QK_ROPE // 2, 1),
                        pltpu.roll(x, QK_ROPE // 2, 1))
    return x * c_tab + partner * s_tab


def _front_kernel(x_ref, ropec_ref, ropes_ref, ang_ref, win_ref, poolw_ref, pscale_ref, cdw_ref, cdwb_ref,
                  clng_ref, clnb_ref, cpw_ref, cpwb_ref, qag_ref, wuq_ref, kvag_ref, wukv_ref, qng_ref,
                  kng_ref, poutg_ref, coutg_ref,
                  mix_ref, q_ref, k_ref, v_ref, pool_ext, conv_ext):
    j = pl.program_id(1)
    tm = x_ref.shape[0]

    x = x_ref[...]
    h = x * _rms(x, D_MODEL) * ang_ref[...]
    z = _dot(h.astype(_bf16), win_ref[...])

    u = z[:, 0:POOL_W]

    @pl.when(j == 0)
    def _():
        pool_ext[0:POOL_HALO, :] = jnp.zeros((POOL_HALO, POOL_W), _f32)
        conv_ext[0:CONV_HALO, :] = jnp.zeros((CONV_HALO, CONV_W), _f32)

    @pl.when(j > 0)
    def _():
        pool_ext[0:POOL_HALO, :] = pool_ext[tm:tm + POOL_HALO, :]
        conv_ext[0:CONV_HALO, :] = conv_ext[tm:tm + CONV_HALO, :]

    pool_ext[POOL_HALO:POOL_HALO + tm, :] = u
    t_pos = (j * tm + lax.broadcasted_iota(jnp.int32, (tm, LANES), 0) + 1).astype(_f32)
    lane = lax.broadcasted_iota(jnp.int32, (tm, LANES), 1)
    pooled_halves = []
    for half, (w_lo, w_hi) in enumerate(((POOL_WINDOWS[0], POOL_WINDOWS[1]),
                                         (POOL_WINDOWS[2], POOL_WINDOWS[3]))):
        cols = slice(half * LANES, (half + 1) * LANES)
        s_lo = pool_ext[POOL_HALO:POOL_HALO + tm, cols]
        for d in range(1, w_lo):
            s_lo = s_lo + pool_ext[POOL_HALO - d:POOL_HALO - d + tm, cols]
        s_hi = s_lo
        for d in range(w_lo, w_hi):
            s_hi = s_hi + pool_ext[POOL_HALO - d:POOL_HALO - d + tm, cols]
        first = lane < POOL_GC
        win_sum = jnp.where(first, s_lo, s_hi)
        cnt = jnp.minimum(t_pos, jnp.where(first, float(w_lo), float(w_hi)))
        pooled_halves.append(win_sum / cnt - u[:, cols])
    pooled = jnp.concatenate(pooled_halves, axis=-1)
    y_pool = _dot(pooled.astype(_bf16), poolw_ref[...]) * pscale_ref[...]
    y_pool = y_pool * _rms(y_pool, POOL_W) * poutg_ref[...]

    a = z[:, POOL_W:POOL_W + CONV_W]
    gate = z[:, POOL_W + CONV_W:POOL_W + 2 * CONV_W]
    conv_ext[CONV_HALO:CONV_HALO + tm, :] = a * jax.nn.sigmoid(gate)
    base = CONV_HALO - (CONV_K - 1)
    acc = conv_ext[base:base + tm, :] * cdw_ref[0:1, :]
    for kk in range(1, CONV_K):
        acc = acc + conv_ext[base + kk:base + kk + tm, :] * cdw_ref[kk:kk + 1, :]
    acc = acc + cdwb_ref[...]
    mu = jnp.mean(acc, axis=-1, keepdims=True)
    cen = acc - mu
    var = jnp.mean(cen * cen, axis=-1, keepdims=True)
    ln = cen * lax.rsqrt(var + EPS) * clng_ref[...] + clnb_ref[...]
    sw = ln * jax.nn.sigmoid(ln)
    y_conv = _dot(sw.astype(_bf16), cpw_ref[...]) + cpwb_ref[...]
    y_conv = y_conv * _rms(y_conv, CONV_W) * coutg_ref[...]

    mix_ref[:, 0:POOL_W] = y_pool.astype(_bf16)
    mix_ref[:, POOL_W:POOL_W + CONV_W] = y_conv.astype(_bf16)

    s1 = POOL_W + 2 * CONV_W
    c_q = z[:, s1:s1 + Q_LORA]
    c_kv = z[:, s1 + Q_LORA:s1 + Q_LORA + KV_LORA]
    k_rope = z[:, s1 + Q_LORA + KV_LORA:s1 + Q_LORA + KV_LORA + LANES]
    c_tab = ropec_ref[...]
    s_tab = ropes_ref[...]

    q_all = _dot((c_q * _rms(c_q, Q_LORA) * qag_ref[...]).astype(_bf16), wuq_ref[...])
    kv_all = _dot((c_kv * _rms(c_kv, KV_LORA) * kvag_ref[...]).astype(_bf16), wukv_ref[...])
    sm_scale = 1.0 / math.sqrt(QK_HEAD)
    kr_ss = jnp.sum(k_rope * k_rope, axis=-1, keepdims=True)
    kr_rot = _rope(k_rope * kng_ref[:, LANES:2 * LANES], c_tab, s_tab)
    for hd in range(N_HEADS):
        qh = q_all[:, hd * HEAD_PAD:(hd + 1) * HEAD_PAD]
        qh = qh * (_rms(qh, QK_HEAD) * sm_scale) * qng_ref[...]
        q_ref[:, hd * HEAD_PAD:hd * HEAD_PAD + LANES] = qh[:, 0:LANES].astype(_bf16)
        q_ref[:, hd * HEAD_PAD + LANES:(hd + 1) * HEAD_PAD] = _rope(qh[:, LANES:], c_tab, s_tab).astype(_bf16)
        kn = kv_all[:, hd * QK_NOPE:(hd + 1) * QK_NOPE]
        k_rs = lax.rsqrt((jnp.sum(kn * kn, axis=-1, keepdims=True) + kr_ss) * (1.0 / QK_HEAD) + EPS)
        k_ref[:, hd * HEAD_PAD:hd * HEAD_PAD + LANES] = (kn * k_rs * kng_ref[:, 0:LANES]).astype(_bf16)
        k_ref[:, hd * HEAD_PAD + LANES:(hd + 1) * HEAD_PAD] = (kr_rot * k_rs).astype(_bf16)
    v_ref[...] = kv_all[:, N_HEADS * QK_NOPE:].astype(_bf16)


def _mixer_front(layer, x, rope_c, rope_s, p):
    B, S, _ = x.shape
    tm = TM_FRONT
    row = lambda w: pl.BlockSpec((None, tm, w), lambda b, j: (b, j, 0))
    names = ("attn_norm_g", "w_in", "pool_w", "pool_scale", "conv_dw", "conv_dw_b", "conv_ln_g", "conv_ln_b",
             "conv_pw", "conv_pw_b", "q_a_g", "w_uq", "kv_a_g", "w_ukv", "q_norm_g", "k_norm_g",
             "pool_out_g", "conv_out_g")
    weights = [p[n] for n in names]
    in_specs = [row(D_MODEL), row(LANES), row(LANES)] + [_layer_spec(w.shape, layer, 2) for w in weights]
    out_shape = (jax.ShapeDtypeStruct((B, S, POOL_W + CONV_W), _bf16),
                 jax.ShapeDtypeStruct((B, S, N_HEADS * HEAD_PAD), _bf16),
                 jax.ShapeDtypeStruct((B, S, N_HEADS * HEAD_PAD), _bf16),
                 jax.ShapeDtypeStruct((B, S, ATTN_W), _bf16))
    out_specs = (row(POOL_W + CONV_W), row(N_HEADS * HEAD_PAD), row(N_HEADS * HEAD_PAD), row(ATTN_W))
    return pl.pallas_call(
        _front_kernel,
        grid=(B, S // tm),
        in_specs=in_specs,
        out_specs=out_specs,
        out_shape=out_shape,
        scratch_shapes=[pltpu.VMEM((tm + POOL_HALO, POOL_W), _f32),
                        pltpu.VMEM((tm + CONV_HALO, CONV_W), _f32)],
        compiler_params=pltpu.CompilerParams(dimension_semantics=("arbitrary", "arbitrary"),
                                             vmem_limit_bytes=VMEM_LIMIT),
        name="mixer_front",
    )(x, rope_c, rope_s, *weights)


def _lane_fold(x, op):
    out = x[:, 0:LANES]
    for g in range(1, x.shape[1] // LANES):
        out = op(out, x[:, g * LANES:(g + 1) * LANES])
    return out


def _attn_kernel(q_ref, k_ref, v_ref, o_ref, s_scr, m_scr, l_scr, acc_scr):
    qi = pl.program_id(2)
    q = q_ref[...]
    tq = q.shape[0]

    def scores(c):
        off = pl.multiple_of(c * tq, tq)
        return lax.dot_general(q, k_ref[pl.ds(off, tq), :], (((1,), (1,)), ((), ())),
                               preferred_element_type=_f32)

    m_scr[...] = jnp.full(m_scr.shape, NEG_BIG, _f32)

    def score_pass(c, carry):
        s = scores(c)
        s_scr[c] = s
        m_scr[...] = jnp.maximum(m_scr[...], _lane_fold(s, jnp.maximum))
        return carry

    lax.fori_loop(0, qi, score_pass, 0)
    s = scores(qi)
    q_chunk = lax.broadcasted_iota(jnp.int32, s.shape, 0) // CHUNK
    k_chunk = lax.broadcasted_iota(jnp.int32, s.shape, 1) // CHUNK
    s = jnp.where(k_chunk <= q_chunk, s, NEG_BIG)
    s_scr[qi] = s
    m_row = jnp.max(jnp.maximum(m_scr[...], _lane_fold(s, jnp.maximum)), axis=-1, keepdims=True)
    m_scr[...] = jnp.broadcast_to(m_row, m_scr.shape)
    l_scr[...] = jnp.zeros(l_scr.shape, _f32)
    acc_scr[...] = jnp.zeros(acc_scr.shape, _f32)

    def value_pass(c, carry):
        off = pl.multiple_of(c * tq, tq)
        m_b = m_scr[...]
        p = jnp.concatenate([jnp.exp(s_scr[c, :, g * LANES:(g + 1) * LANES] - m_b)
                             for g in range(tq // LANES)], axis=-1)
        l_scr[...] += _lane_fold(p, jnp.add)
        acc_scr[...] += _dot(p.astype(_bf16), v_ref[pl.ds(off, tq), :])
        return carry

    lax.fori_loop(0, qi + 1, value_pass, 0)
    o_ref[...] = (acc_scr[...] / jnp.sum(l_scr[...], axis=-1, keepdims=True)).astype(o_ref.dtype)


def _attention(q, k, v):
    B, S, _ = q.shape
    nq = S // TQ
    return pl.pallas_call(
        _attn_kernel,
        grid=(B, N_HEADS, nq),
        in_specs=[pl.BlockSpec((None, TQ, HEAD_PAD), lambda b, h, i: (b, i, h)),
                  pl.BlockSpec((None, S, HEAD_PAD), lambda b, h, i: (b, 0, h)),
                  pl.BlockSpec((None, S, V_DIM), lambda b, h, i: (b, 0, h))],
        out_specs=pl.BlockSpec((None, TQ, V_DIM), lambda b, h, i: (b, i, h)),
        out_shape=jax.ShapeDtypeStruct((B, S, ATTN_W), _bf16),
        scratch_shapes=[pltpu.VMEM((nq, TQ, TQ), _f32), pltpu.VMEM((TQ, LANES), _f32),
                        pltpu.VMEM((TQ, LANES), _f32), pltpu.VMEM((TQ, V_DIM), _f32)],
        compiler_params=pltpu.CompilerParams(dimension_semantics=("arbitrary", "arbitrary", "arbitrary"),
                                             vmem_limit_bytes=VMEM_LIMIT),
        name="attention",
    )(q, k, v)


def _post_kernel(x_ref, mix_ref, att_ref, aog_ref, wo_ref, fng_ref, wr_ref, br_ref,
                 x1_ref, h2_ref, route_ref, rgate_ref, cnt_ref):
    tm = x_ref.shape[0]
    att = att_ref[...].astype(_f32)
    att_n = (att * _rms(att, ATTN_W) * aog_ref[...]).astype(_bf16)
    x1 = x_ref[...] + _dot(mix_ref[...], wo_ref[0:POOL_W + CONV_W, :]) + _dot(att_n, wo_ref[POOL_W + CONV_W:, :])
    x1_ref[...] = x1
    h2 = (x1 * _rms(x1, D_MODEL) * fng_ref[...]).astype(_bf16)
    h2_ref[...] = h2

    logits = _dot(h2, wr_ref[...]) + br_ref[...]
    lane = lax.broadcasted_iota(jnp.int32, logits.shape, 1)
    lane_f = lane.astype(_f32)
    work = logits
    sel = jnp.zeros(logits.shape, _f32)
    onehots, vals, idxs = [], [], []
    for _ in range(TOP_K):
        mx = jnp.max(work, axis=-1, keepdims=True)
        idx_f = jnp.min(jnp.where(work == mx, lane_f, float(LANES)), axis=-1, keepdims=True)
        idx = idx_f.astype(jnp.int32)
        hot = lane == idx
        onehots.append(hot)
        vals.append(mx)
        idxs.append(idx)
        sel = jnp.where(hot, 1.0, sel)
        work = jnp.where(hot, -jnp.inf, work)
    exps = [jnp.exp(vk - vals[0]) for vk in vals]
    denom = exps[0] + exps[1] + exps[2] + exps[3]

    nt = tm // ROUTE_TILE
    r_i = lax.broadcasted_iota(jnp.int32, (ROUTE_TILE, ROUTE_TILE), 0)
    c_i = lax.broadcasted_iota(jnp.int32, (ROUTE_TILE, ROUTE_TILE), 1)
    tri = jnp.where(c_i < r_i, 1.0, 0.0).astype(_bf16)
    upper = jnp.where(r_i < c_i, 1.0, 0.0).astype(_bf16)
    sel_b = sel.astype(_bf16)
    ranks, cnts = [], []
    for s in range(nt):
        rows = slice(s * ROUTE_TILE, (s + 1) * ROUTE_TILE)
        ranks.append(_dot(tri, sel_b[rows, :]))
        cnts.append(jnp.sum(sel[rows, :], axis=0, keepdims=True))
    cnt = jnp.concatenate(cnts, axis=0)
    cnt_ref[...] = cnt
    chunks = jnp.floor((cnt + (ROW_CHUNK - 1)) * (1.0 / ROW_CHUNK))
    boff = _dot(chunks.astype(_bf16), upper) * float(ROW_CHUNK)
    slot = jnp.concatenate(
        [ranks[s] + jnp.broadcast_to(boff[s:s + 1, :], (ROUTE_TILE, LANES)) for s in range(nt)], axis=0)

    route = jnp.zeros(logits.shape, jnp.int32)
    rgate = jnp.zeros(logits.shape, _f32)
    for kk in range(TOP_K):
        col = jnp.sum(jnp.where(onehots[kk], slot, 0.0), axis=-1, keepdims=True).astype(jnp.int32)
        route = jnp.where(lane == kk, idxs[kk], route)
        route = jnp.where(lane == TOP_K + kk, col, route)
        rgate = jnp.where(lane == kk, exps[kk] / denom, rgate)
    route_ref[...] = route
    rgate_ref[...] = rgate


def _post_mix(layer, x, mix, att, p):
    T = x.shape[0]
    tm = TM_POST
    row = lambda w: pl.BlockSpec((tm, w), lambda i: (i, 0))
    names = ("attn_out_g", "w_o", "ffn_norm_g", "w_router", "b_router")
    weights = [p[n] for n in names]
    out_shape = (jax.ShapeDtypeStruct((T, D_MODEL), _f32), jax.ShapeDtypeStruct((T, D_MODEL), _bf16),
                 jax.ShapeDtypeStruct((T, LANES), jnp.int32), jax.ShapeDtypeStruct((T, LANES), _f32),
                 jax.ShapeDtypeStruct((T // ROUTE_TILE, LANES), _f32))
    return pl.pallas_call(
        _post_kernel,
        grid=(T // tm,),
        in_specs=[row(D_MODEL), row(POOL_W + CONV_W), row(ATTN_W)] + [_layer_spec(w.shape, layer, 1) for w in weights],
        out_specs=(row(D_MODEL), row(D_MODEL), row(LANES), row(LANES),
                   pl.BlockSpec((tm // ROUTE_TILE, LANES), lambda i: (i, 0))),
        out_shape=out_shape,
        compiler_params=pltpu.CompilerParams(dimension_semantics=("arbitrary",), vmem_limit_bytes=VMEM_LIMIT),
        name="post_mix",
    )(x, mix, att, *weights)


def _chunk_loops(n_chunks, boff, base, tile, copy_fn):
    for e in range(N_EXPERTS):
        idx = tile * N_EXPERTS + e

        def body(c, carry, idx=idx):
            copy_fn(boff[idx] + c * ROW_CHUNK, base[idx] + c * ROW_CHUNK)
            return carry

        lax.fori_loop(0, n_chunks[idx], body, 0)


def _dispatch_kernel(n_chunks, boff, base, n_total, pad_start, pad_chunks,
                     slot_ref, h2_ref, xs_hbm, stage, zeros, sems, zsem):
    i = pl.program_id(0)
    n_steps = pl.num_programs(0)
    cur = i % 2

    def chunk_copy(slot, stage_row, sorted_row):
        return pltpu.make_async_copy(stage.at[slot, pl.ds(stage_row, ROW_CHUNK)],
                                     xs_hbm.at[pl.ds(sorted_row, ROW_CHUNK)], sems.at[slot])

    def wait_tile(tile, slot):
        def body(c, carry):
            chunk_copy(slot, 0, 0).wait()
            return carry
        lax.fori_loop(0, n_total[tile], body, 0)

    @pl.when(i == 0)
    def _():
        zeros[...] = jnp.zeros(zeros.shape, _f32)

        def zero_copy(row):
            return pltpu.make_async_copy(zeros, xs_hbm.at[pl.ds(row, ROW_CHUNK)], zsem)

        for e in range(N_EXPERTS):
            def start(c, carry, e=e):
                zero_copy(pad_start[e] + c * ROW_CHUNK).start()
                return carry
            lax.fori_loop(0, pad_chunks[e], start, 0)
        for e in range(N_EXPERTS):
            def wait(c, carry, e=e):
                zero_copy(pad_start[e] + c * ROW_CHUNK).wait()
                return carry
            lax.fori_loop(0, pad_chunks[e], wait, 0)

    slots = slot_ref[...]
    row_i = lax.broadcasted_iota(jnp.int32, (STAGE_ROWS, ROUTE_TILE), 0)
    place = jnp.zeros((STAGE_ROWS, ROUTE_TILE), _f32)
    for kk in range(TOP_K):
        place = place + jnp.where(row_i == slots[kk:kk + 1, :], 1.0, 0.0)
    staged = _dot(place.astype(_bf16), h2_ref[...])
    for j in range(D_MODEL // LANES):
        stage[cur, :, j, :] = staged[:, j * LANES:(j + 1) * LANES]

    @pl.when(i > 0)
    def _():
        wait_tile(i - 1, 1 - cur)

    _chunk_loops(n_chunks, boff, base, i, lambda sr, dr: chunk_copy(cur, sr, dr).start())

    @pl.when(i == n_steps - 1)
    def _():
        wait_tile(i, cur)


def _dispatch(h2, slots_t, sched, n_rows):
    T = h2.shape[0]
    grid_spec = pltpu.PrefetchScalarGridSpec(
        num_scalar_prefetch=6,
        grid=(T // ROUTE_TILE,),
        in_specs=[pl.BlockSpec((None, SUBLANES, ROUTE_TILE), lambda i, *_: (i, 0, 0)),
                  pl.BlockSpec((ROUTE_TILE, D_MODEL), lambda i, *_: (i, 0))],
        out_specs=pl.BlockSpec(memory_space=pl.ANY),
        scratch_shapes=[pltpu.VMEM((2, STAGE_ROWS, D_MODEL // LANES, LANES), _f32),
                        pltpu.VMEM((ROW_CHUNK, D_MODEL // LANES, LANES), _f32),
                        pltpu.SemaphoreType.DMA((2,)), pltpu.SemaphoreType.DMA(())],
    )
    return pl.pallas_call(
        _dispatch_kernel,
        grid_spec=grid_spec,
        out_shape=jax.ShapeDtypeStruct((n_rows, D_MODEL // LANES, LANES), _f32),
        compiler_params=pltpu.CompilerParams(dimension_semantics=("arbitrary",), vmem_limit_bytes=VMEM_LIMIT),
        name="dispatch",
    )(sched["n_chunks"], sched["boff"], sched["base"], sched["n_total"], sched["pad_start"], sched["pad_chunks"],
      slots_t, h2)


def _expert_kernel(blk_exp, blk_valid, xs_ref, wgu_ref, bgu_ref, wd_ref, bd_ref, y_ref):
    i = pl.program_id(0)
    valid = blk_valid[i]
    n_slab = D_MODEL // LANES

    @pl.when(valid > 0)
    def _():
        x = jnp.concatenate([xs_ref[:, j, :] for j in range(n_slab)], axis=-1).astype(_bf16)
        y = jnp.zeros((EXPERT_BLOCK, D_MODEL), _f32)
        for c in range(D_FF // FF_CHUNK):
            lo = c * FF_CHUNK
            hg = _dot(x, wgu_ref[:, lo:lo + FF_CHUNK]) + bgu_ref[:, lo:lo + FF_CHUNK]
            hl = _dot(x, wgu_ref[:, D_FF + lo:D_FF + lo + FF_CHUNK]) + bgu_ref[:, D_FF + lo:D_FF + lo + FF_CHUNK]
            g = jnp.minimum(hg, SWIGLU_LIMIT)
            lin = jnp.clip(hl, -SWIGLU_LIMIT, SWIGLU_LIMIT)
            act = g * jax.nn.sigmoid(SWIGLU_ALPHA * g) * (lin + 1.0)
            y = y + _dot(act.astype(_bf16), wd_ref[lo:lo + FF_CHUNK, :])
        y = y + bd_ref[...]
        for j in range(n_slab):
            y_ref[:, j, :] = y[:, j * LANES:(j + 1) * LANES]

    @pl.when(valid == 0)
    def _():
        y_ref[...] = jnp.zeros(y_ref.shape, _f32)


def _experts(layer, xs, blk_exp, blk_valid, w_gu, b_gu, w_down, b_down):
    n_rows = xs.shape[0]
    nb = n_rows // EXPERT_BLOCK
    n_slab = D_MODEL // LANES
    xmap = lambda i, be, bv: (jnp.where(bv[i] > 0, i, 0), 0, 0)
    grid_spec = pltpu.PrefetchScalarGridSpec(
        num_scalar_prefetch=2,
        grid=(nb,),
        in_specs=[pl.BlockSpec((EXPERT_BLOCK, n_slab, LANES), xmap),
                  pl.BlockSpec((None, None, D_MODEL, 2 * D_FF), lambda i, be, bv: (layer, be[i], 0, 0)),
                  pl.BlockSpec((None, None, 1, 2 * D_FF), lambda i, be, bv: (layer, be[i], 0, 0)),
                  pl.BlockSpec((None, None, D_FF, D_MODEL), lambda i, be, bv: (layer, be[i], 0, 0)),
                  pl.BlockSpec((None, None, 1, D_MODEL), lambda i, be, bv: (layer, be[i], 0, 0))],
        out_specs=pl.BlockSpec((EXPERT_BLOCK, n_slab, LANES), lambda i, be, bv: (i, 0, 0)),
    )
    return pl.pallas_call(
        _expert_kernel,
        grid_spec=grid_spec,
        out_shape=jax.ShapeDtypeStruct((n_rows, n_slab, LANES), _f32),
        compiler_params=pltpu.CompilerParams(dimension_semantics=("arbitrary",), vmem_limit_bytes=VMEM_LIMIT),
        name="experts",
    )(blk_exp, blk_valid, xs, w_gu, b_gu, w_down, b_down)


def _combine_kernel(n_chunks, boff, base, n_total, x1_ref, route_ref, gate_ref, ys_hbm, out_ref, stage, sems):
    i = pl.program_id(0)
    n_steps = pl.num_programs(0)
    cur = i % 2

    def chunk_copy(slot, stage_row, sorted_row):
        return pltpu.make_async_copy(ys_hbm.at[pl.ds(sorted_row, ROW_CHUNK)],
                                     stage.at[slot, pl.ds(stage_row, ROW_CHUNK)], sems.at[slot])

    def fetch_tile(tile, slot):
        _chunk_loops(n_chunks, boff, base, tile, lambda sr, dr: chunk_copy(slot, sr, dr).start())

    @pl.when(i == 0)
    def _():
        stage[...] = jnp.zeros(stage.shape, _f32)
        fetch_tile(0, 0)

    @pl.when(i + 1 < n_steps)
    def _():
        fetch_tile(i + 1, 1 - cur)

    def wait_one(c, carry):
        chunk_copy(cur, 0, 0).wait()
        return carry

    lax.fori_loop(0, n_total[i], wait_one, 0)

    rows = jnp.concatenate([stage[cur, :, j, :] for j in range(D_MODEL // LANES)], axis=-1).astype(_bf16)
    route = route_ref[...]
    gates = gate_ref[...]
    lane_i = lax.broadcasted_iota(jnp.int32, (ROUTE_TILE, STAGE_ROWS), 1)
    weight = jnp.zeros((ROUTE_TILE, STAGE_ROWS), _f32)
    for kk in range(TOP_K):
        weight = weight + jnp.where(lane_i == route[:, TOP_K + kk:TOP_K + kk + 1], gates[:, kk:kk + 1], 0.0)
    out_ref[...] = x1_ref[...] + _dot(weight.astype(_bf16), rows)


def _combine(x1, route, rgate, ys, sched):
    T = x1.shape[0]
    tc = ROUTE_TILE
    grid_spec = pltpu.PrefetchScalarGridSpec(
        num_scalar_prefetch=4,
        grid=(T // tc,),
        in_specs=[pl.BlockSpec((tc, D_MODEL), lambda i, *_: (i, 0)),
                  pl.BlockSpec((tc, LANES), lambda i, *_: (i, 0)),
                  pl.BlockSpec((tc, LANES), lambda i, *_: (i, 0)),
                  pl.BlockSpec(memory_space=pl.ANY)],
        out_specs=pl.BlockSpec((tc, D_MODEL), lambda i, *_: (i, 0)),
        scratch_shapes=[pltpu.VMEM((2, STAGE_ROWS, D_MODEL // LANES, LANES), _f32),
                        pltpu.SemaphoreType.DMA((2,))],
    )
    return pl.pallas_call(
        _combine_kernel,
        grid_spec=grid_spec,
        out_shape=jax.ShapeDtypeStruct((T, D_MODEL), _f32),
        compiler_params=pltpu.CompilerParams(dimension_semantics=("arbitrary",), vmem_limit_bytes=VMEM_LIMIT),
        name="combine",
    )(sched["n_chunks"], sched["boff"], sched["base"], sched["n_total"], x1, route, rgate, ys)


def _prepare_params(w):
    L = w["w_in"].shape[0]
    p = {}
    row = lambda a: a.reshape(L, 1, a.shape[-1])
    p["attn_norm_g"] = row(w["attn_norm_g"])
    p["w_in"] = jnp.pad(w["w_in"], ((0, 0), (0, 0), (0, D_IN_PAD - w["w_in"].shape[-1]))).astype(_bf16)
    eye = jnp.eye(len(POOL_WINDOWS), dtype=_f32)
    p["pool_w"] = jnp.einsum("lgcd,gh->lgchd", w["pool_w"], eye).reshape(L, POOL_W, POOL_W).astype(_bf16)
    p["pool_scale"] = row(w["pool_scale"])
    p["conv_dw"] = w["conv_dw"]
    p["conv_dw_b"] = row(w["conv_dw_b"])
    p["conv_ln_g"] = row(w["conv_ln_g"])
    p["conv_ln_b"] = row(w["conv_ln_b"])
    p["conv_pw"] = w["conv_pw"].astype(_bf16)
    p["conv_pw_b"] = row(w["conv_pw_b"])
    p["q_a_g"] = row(w["q_a_g"])
    wuq = w["w_uq"].reshape(L, Q_LORA, N_HEADS, QK_HEAD)
    p["w_uq"] = jnp.pad(wuq, ((0, 0), (0, 0), (0, 0), (0, HEAD_PAD - QK_HEAD))).reshape(
        L, Q_LORA, N_HEADS * HEAD_PAD).astype(_bf16)
    p["kv_a_g"] = row(w["kv_a_g"])
    wukv = w["w_ukv"].reshape(L, KV_LORA, N_HEADS, QK_NOPE + V_DIM)
    p["w_ukv"] = jnp.concatenate([wukv[..., :QK_NOPE].reshape(L, KV_LORA, N_HEADS * QK_NOPE),
                                  wukv[..., QK_NOPE:].reshape(L, KV_LORA, N_HEADS * V_DIM)], axis=-1).astype(_bf16)
    pad_head = lambda g: jnp.pad(g, ((0, 0), (0, HEAD_PAD - QK_HEAD))).reshape(L, 1, HEAD_PAD)
    p["q_norm_g"] = pad_head(w["q_norm_g"])
    p["k_norm_g"] = pad_head(w["k_norm_g"])
    p["pool_out_g"] = row(w["pool_out_g"])
    p["conv_out_g"] = row(w["conv_out_g"])
    p["attn_out_g"] = row(w["attn_out_g"])
    p["w_o"] = w["w_o"].astype(_bf16)
    p["ffn_norm_g"] = row(w["ffn_norm_g"])
    p["w_router"] = jnp.pad(w["w_router"], ((0, 0), (0, 0), (0, LANES - N_EXPERTS))).astype(_bf16)
    p["b_router"] = jnp.pad(w["b_router"], ((0, 0), (0, LANES - N_EXPERTS)),
                            constant_values=NEG_BIG).reshape(L, 1, LANES)
    E = w["w_gu"].shape[1]
    wgu = w["w_gu"].reshape(L, E, D_MODEL, D_FF, 2)
    p["w_gu"] = jnp.concatenate([wgu[..., 0], wgu[..., 1]], axis=-1).astype(_bf16)
    bgu = w["b_gu"].reshape(L, E, D_FF, 2)
    p["b_gu"] = jnp.concatenate([bgu[..., 0], bgu[..., 1]], axis=-1).reshape(L, E, 1, 2 * D_FF)
    p["w_down"] = w["w_down"].astype(_bf16)
    p["b_down"] = w["b_down"].reshape(L, E, 1, D_MODEL)
    return p


def _rope_tables(positions):
    inv_freq = 1.0 / (ROPE_THETA ** (jnp.arange(0, QK_ROPE, 2, dtype=_f32) / QK_ROPE))
    ang = positions.astype(_f32)[..., None] * inv_freq
    cos, sin = jnp.cos(ang), jnp.sin(ang)
    zeros = jnp.zeros(cos.shape[:-1] + (LANES - QK_ROPE,), _f32)
    return (jnp.concatenate([cos, cos, zeros], axis=-1), jnp.concatenate([-sin, sin, zeros], axis=-1))


def _routing_schedule(cnt_f, n_tokens):
    n = cnt_f[:, 0:N_EXPERTS].astype(jnp.int32)
    counts = jnp.sum(n, axis=0)
    spare = ROW_CHUNK - 1
    padded = jnp.where(counts > 0, (counts + spare + EXPERT_BLOCK - 1) // EXPERT_BLOCK * EXPERT_BLOCK, 0)
    pend = jnp.cumsum(padded)
    pstart = pend - padded
    base = pstart[None, :] + jnp.cumsum(n, axis=0) - n
    n_chunks = (n + ROW_CHUNK - 1) // ROW_CHUNK
    boff = (jnp.cumsum(n_chunks, axis=1) - n_chunks) * ROW_CHUNK
    max_rows = n_tokens * TOP_K + N_EXPERTS * (spare + EXPERT_BLOCK - 1) + ROW_CHUNK
    n_blocks = -(-max_rows // EXPERT_BLOCK)
    blk_row = jnp.arange(n_blocks, dtype=jnp.int32) * EXPERT_BLOCK
    blk_exp = jnp.minimum(jnp.sum((blk_row[:, None] >= pend[None, :]).astype(jnp.int32), axis=1), N_EXPERTS - 1)
    onehot = (blk_exp[:, None] == jnp.arange(N_EXPERTS, dtype=jnp.int32)[None, :]).astype(jnp.int32)
    seg_end = jnp.sum(onehot * (pstart + counts)[None, :], axis=1)
    blk_valid = jnp.where(blk_row < pend[-1], jnp.clip(seg_end - blk_row, 0, EXPERT_BLOCK), 0)
    i32 = lambda a: a.astype(jnp.int32)
    sched = dict(n_chunks=i32(n_chunks).reshape(-1), boff=i32(boff).reshape(-1), base=i32(base).reshape(-1),
                 n_total=i32(jnp.sum(n_chunks, axis=1)), pad_start=i32(pstart + counts),
                 pad_chunks=i32((padded - counts + ROW_CHUNK - 1) // ROW_CHUNK))
    return sched, i32(blk_exp), i32(blk_valid), n_blocks * EXPERT_BLOCK


def kernel(x, positions, attn_norm_g, w_in, pool_w, pool_scale, conv_dw, conv_dw_b, conv_ln_g, conv_ln_b, conv_pw, conv_pw_b, q_a_g, w_uq, kv_a_g, w_ukv, q_norm_g, k_norm_g, pool_out_g, conv_out_g, attn_out_g, w_o, ffn_norm_g, w_router, b_router, w_gu, b_gu, w_down, b_down):
    B, S, D = x.shape
    T = B * S
    depth = w_in.shape[0]
    p = _prepare_params(dict(
        attn_norm_g=attn_norm_g, w_in=w_in, pool_w=pool_w, pool_scale=pool_scale, conv_dw=conv_dw,
        conv_dw_b=conv_dw_b, conv_ln_g=conv_ln_g, conv_ln_b=conv_ln_b, conv_pw=conv_pw, conv_pw_b=conv_pw_b,
        q_a_g=q_a_g, w_uq=w_uq, kv_a_g=kv_a_g, w_ukv=w_ukv, q_norm_g=q_norm_g, k_norm_g=k_norm_g,
        pool_out_g=pool_out_g, conv_out_g=conv_out_g, attn_out_g=attn_out_g, w_o=w_o, ffn_norm_g=ffn_norm_g,
        w_router=w_router, b_router=b_router, w_gu=w_gu, b_gu=b_gu, w_down=w_down, b_down=b_down))
    rope_c, rope_s = _rope_tables(positions)
    for layer in range(depth):
        mix, q, k, v = _mixer_front(layer, x, rope_c, rope_s, p)
        att = _attention(q, k, v)
        x1, h2, route, rgate, cnt = _post_mix(layer, x.reshape(T, D), mix.reshape(T, -1), att.reshape(T, -1), p)
        sched, blk_exp, blk_valid, n_rows = _routing_schedule(cnt, T)
        slots_t = route[:, TOP_K:2 * TOP_K].reshape(T // ROUTE_TILE, ROUTE_TILE, TOP_K).transpose(0, 2, 1)
        slots_t = jnp.pad(slots_t, ((0, 0), (0, SUBLANES - TOP_K), (0, 0)), constant_values=-1)
        xs = _dispatch(h2, slots_t, sched, n_rows)
        ys = _experts(layer, xs, blk_exp, blk_valid, p["w_gu"], p["b_gu"], p["w_down"], p["b_down"])
        x = _combine(x1, route, rgate, ys, sched).reshape(B, S, D)
    return x
```

```python
import functools
import math

import jax
import jax.numpy as jnp
from jax import lax
from jax.experimental import pallas as pl
from jax.experimental.pallas import tpu as pltpu

D_MODEL = 1024
CHUNK = 64
POOL_W = 256
POOL_WINDOWS = (2, 4, 8, 16)
POOL_GC = 64
CONV_W = 256
CONV_K = 31
N_HEADS = 4
QK_NOPE = 128
QK_ROPE = 64
QK_HEAD = QK_NOPE + QK_ROPE
V_DIM = 128
Q_LORA = 256
KV_LORA = 128
ATTN_W = N_HEADS * V_DIM
ROPE_THETA = 10000.0
N_EXPERTS = 32
TOP_K = 4
D_FF = D_MODEL
SWIGLU_LIMIT = 7.0
SWIGLU_ALPHA = 1.702
EPS = 1e-6

LANES = 128
HEAD_PAD = 2 * LANES
D_IN_PAD = 1280
POOL_HALO = 16
CONV_HALO = 32
NEG_BIG = -1e30

TM_FRONT = 512
TQ = 512
TM_POST = 1024
ROUTE_TILE = 128
ROW_CHUNK = 8
SUBLANES = 8
STAGE_ROWS = -(-(ROUTE_TILE * TOP_K + N_EXPERTS * (ROW_CHUNK - 1)) // LANES) * LANES
EXPERT_BLOCK = 256
FF_CHUNK = 512
VMEM_LIMIT = 56 * 1024 * 1024

_bf16 = jnp.bfloat16
_f32 = jnp.float32


def _dot(a, b):
    return jnp.dot(a, b, preferred_element_type=_f32)


def _rms(x, width):
    return lax.rsqrt(jnp.sum(x * x, axis=-1, keepdims=True) * (1.0 / width) + EPS)


def _layer_spec(shape, layer, n_grid):
    nd = len(shape)
    block = (None,) + tuple(shape[1:])
    if n_grid == 1:
        return pl.BlockSpec(block, lambda i: (layer,) + (0,) * (nd - 1))
    return pl.BlockSpec(block, lambda i, j: (layer,) + (0,) * (nd - 1))


def _rope(x, c_tab, s_tab):
    lane = lax.broadcasted_iota(jnp.int32, x.shape, 1)
    partner = jnp.where(lane < QK_ROPE // 2, pltpu.roll(x, LANES - QK_ROPE // 2, 1),
                        pltpu.roll(x, QK_ROPE // 2, 1))
    return x * c_tab + partner * s_tab


def _front_kernel(x_ref, ropec_ref, ropes_ref, ang_ref, win_ref, poolw_ref, pscale_ref, cdw_ref, cdwb_ref,
                  clng_ref, clnb_ref, cpw_ref, cpwb_ref, qag_ref, wuq_ref, kvag_ref, wukv_ref, qng_ref,
                  kng_ref, poutg_ref, coutg_ref,
                  mix_ref, q_ref, k_ref, v_ref, pool_ext, conv_ext):
    j = pl.program_id(1)
    tm = x_ref.shape[0]

    x = x_ref[...]
    h = x * _rms(x, D_MODEL) * ang_ref[...]
    z = _dot(h.astype(_bf16), win_ref[...])

    u = z[:, 0:POOL_W]

    @pl.when(j == 0)
    def _():
        pool_ext[0:POOL_HALO, :] = jnp.zeros((POOL_HALO, POOL_W), _f32)
        conv_ext[0:CONV_HALO, :] = jnp.zeros((CONV_HALO, CONV_W), _f32)

    @pl.when(j > 0)
    def _():
        pool_ext[0:POOL_HALO, :] = pool_ext[tm:tm + POOL_HALO, :]
        conv_ext[0:CONV_HALO, :] = conv_ext[tm:tm + CONV_HALO, :]

    pool_ext[POOL_HALO:POOL_HALO + tm, :] = u
    t_pos = (j * tm + lax.broadcasted_iota(jnp.int32, (tm, LANES), 0) + 1).astype(_f32)
    lane = lax.broadcasted_iota(jnp.int32, (tm, LANES), 1)
    pooled_halves = []
    for half, (w_lo, w_hi) in enumerate(((POOL_WINDOWS[0], POOL_WINDOWS[1]),
                                         (POOL_WINDOWS[2], POOL_WINDOWS[3]))):
        cols = slice(half * LANES, (half + 1) * LANES)
        s_lo = pool_ext[POOL_HALO:POOL_HALO + tm, cols]
        for d in range(1, w_lo):
            s_lo = s_lo + pool_ext[POOL_HALO - d:POOL_HALO - d + tm, cols]
        s_hi = s_lo
        for d in range(w_lo, w_hi):
            s_hi = s_hi + pool_ext[POOL_HALO - d:POOL_HALO - d + tm, cols]
        first = lane < POOL_GC
        win_sum = jnp.where(first, s_lo, s_hi)
        cnt = jnp.minimum(t_pos, jnp.where(first, float(w_lo), float(w_hi)))
        pooled_halves.append(win_sum / cnt - u[:, cols])
    pooled = jnp.concatenate(pooled_halves, axis=-1)
    y_pool = _dot(pooled.astype(_bf16), poolw_ref[...]) * pscale_ref[...]
    y_pool = y_pool * _rms(y_pool, POOL_W) * poutg_ref[...]

    a = z[:, POOL_W:POOL_W + CONV_W]
    gate = z[:, POOL_W + CONV_W:POOL_W + 2 * CONV_W]
    conv_ext[CONV_HALO:CONV_HALO + tm, :] = a * jax.nn.sigmoid(gate)
    base = CONV_HALO - (CONV_K - 1)
    acc = conv_ext[base:base + tm, :] * cdw_ref[0:1, :]
    for kk in range(1, CONV_K):
        acc = acc + conv_ext[base + kk:base + kk + tm, :] * cdw_ref[kk:kk + 1, :]
    acc = acc + cdwb_ref[...]
    mu = jnp.mean(acc, axis=-1, keepdims=True)
    cen = acc - mu
    var = jnp.mean(cen * cen, axis=-1, keepdims=True)
    ln = cen * lax.rsqrt(var + EPS) * clng_ref[...] + clnb_ref[...]
    sw = ln * jax.nn.sigmoid(ln)
    y_conv = _dot(sw.astype(_bf16), cpw_ref[...]) + cpwb_ref[...]
    y_conv = y_conv * _rms(y_conv, CONV_W) * coutg_ref[...]

    mix_ref[:, 0:POOL_W] = y_pool.astype(_bf16)
    mix_ref[:, POOL_W:POOL_W + CONV_W] = y_conv.astype(_bf16)

    s1 = POOL_W + 2 * CONV_W
    c_q = z[:, s1:s1 + Q_LORA]
    c_kv = z[:, s1 + Q_LORA:s1 + Q_LORA + KV_LORA]
    k_rope = z[:, s1 + Q_LORA + KV_LORA:s1 + Q_LORA + KV_LORA + LANES]
    c_tab = ropec_ref[...]
    s_tab = ropes_ref[...]

    q_all = _dot((c_q * _rms(c_q, Q_LORA) * qag_ref[...]).astype(_bf16), wuq_ref[...])
    kv_all = _dot((c_kv * _rms(c_kv, KV_LORA) * kvag_ref[...]).astype(_bf16), wukv_ref[...])
    sm_scale = 1.0 / math.sqrt(QK_HEAD)
    kr_ss = jnp.sum(k_rope * k_rope, axis=-1, keepdims=True)
    kr_rot = _rope(k_rope * kng_ref[:, LANES:2 * LANES], c_tab, s_tab)
    for hd in range(N_HEADS):
        qh = q_all[:, hd * HEAD_PAD:(hd + 1) * HEAD_PAD]
        qh = qh * (_rms(qh, QK_HEAD) * sm_scale) * qng_ref[...]
        q_ref[:, hd * HEAD_PAD:hd * HEAD_PAD + LANES] = qh[:, 0:LANES].astype(_bf16)
        q_ref[:, hd * HEAD_PAD + LANES:(hd + 1) * HEAD_PAD] = _rope(qh[:, LANES:], c_tab, s_tab).astype(_bf16)
        kn = kv_all[:, hd * QK_NOPE:(hd + 1) * QK_NOPE]
        k_rs = lax.rsqrt((jnp.sum(kn * kn, axis=-1, keepdims=True) + kr_ss) * (1.0 / QK_HEAD) + EPS)
        k_ref[:, hd * HEAD_PAD:hd * HEAD_PAD + LANES] = (kn * k_rs * kng_ref[:, 0:LANES]).astype(_bf16)
        k_ref[:, hd * HEAD_PAD + LANES:(hd + 1) * HEAD_PAD] = (kr_rot * k_rs).astype(_bf16)
    v_ref[...] = kv_all[:, N_HEADS * QK_NOPE:].astype(_bf16)


def _mixer_front(layer, x, rope_c, rope_s, p):
    B, S, _ = x.shape
    tm = TM_FRONT
    row = lambda w: pl.BlockSpec((None, tm, w), lambda b, j: (b, j, 0))
    names = ("attn_norm_g", "w_in", "pool_w", "pool_scale", "conv_dw", "conv_dw_b", "conv_ln_g", "conv_ln_b",
             "conv_pw", "conv_pw_b", "q_a_g", "w_uq", "kv_a_g", "w_ukv", "q_norm_g", "k_norm_g",
             "pool_out_g", "conv_out_g")
    weights = [p[n] for n in names]
    in_specs = [row(D_MODEL), row(LANES), row(LANES)] + [_layer_spec(w.shape, layer, 2) for w in weights]
    out_shape = (jax.ShapeDtypeStruct((B, S, POOL_W + CONV_W), _bf16),
                 jax.ShapeDtypeStruct((B, S, N_HEADS * HEAD_PAD), _bf16),
                 jax.ShapeDtypeStruct((B, S, N_HEADS * HEAD_PAD), _bf16),
                 jax.ShapeDtypeStruct((B, S, ATTN_W), _bf16))
    out_specs = (row(POOL_W + CONV_W), row(N_HEADS * HEAD_PAD), row(N_HEADS * HEAD_PAD), row(ATTN_W))
    return pl.pallas_call(
        _front_kernel,
        grid=(B, S // tm),
        in_specs=in_specs,
        out_specs=out_specs,
        out_shape=out_shape,
        scratch_shapes=[pltpu.VMEM((tm + POOL_HALO, POOL_W), _f32),
                        pltpu.VMEM((tm + CONV_HALO, CONV_W), _f32)],
        compiler_params=pltpu.CompilerParams(dimension_semantics=("arbitrary", "arbitrary"),
                                             vmem_limit_bytes=VMEM_LIMIT),
        name="mixer_front",
    )(x, rope_c, rope_s, *weights)


def _lane_fold(x, op):
    out = x[:, 0:LANES]
    for g in range(1, x.shape[1] // LANES):
        out = op(out, x[:, g * LANES:(g + 1) * LANES])
    return out


def _attn_kernel(q_ref, k_ref, v_ref, o_ref, s_scr, m_scr, l_scr, acc_scr):
    qi = pl.program_id(2)
    q = q_ref[...]
    tq = q.shape[0]

    def scores(c):
        off = pl.multiple_of(c * tq, tq)
        return lax.dot_general(q, k_ref[pl.ds(off, tq), :], (((1,), (1,)), ((), ())),
                               preferred_element_type=_f32)

    m_scr[...] = jnp.full(m_scr.shape, NEG_BIG, _f32)

    def score_pass(c, carry):
        s = scores(c)
        s_scr[c] = s
        m_scr[...] = jnp.maximum(m_scr[...], _lane_fold(s, jnp.maximum))
        return carry

    lax.fori_loop(0, qi, score_pass, 0)
    s = scores(qi)
    q_chunk = lax.broadcasted_iota(jnp.int32, s.shape, 0) // CHUNK
    k_chunk = lax.broadcasted_iota(jnp.int32, s.shape, 1) // CHUNK
    s = jnp.where(k_chunk <= q_chunk, s, NEG_BIG)
    s_scr[qi] = s
    m_row = jnp.max(jnp.maximum(m_scr[...], _lane_fold(s, jnp.maximum)), axis=-1, keepdims=True)
    m_scr[...] = jnp.broadcast_to(m_row, m_scr.shape)
    l_scr[...] = jnp.zeros(l_scr.shape, _f32)
    acc_scr[...] = jnp.zeros(acc_scr.shape, _f32)

    def value_pass(c, carry):
        off = pl.multiple_of(c * tq, tq)
        m_b = m_scr[...]
        p = jnp.concatenate([jnp.exp(s_scr[c, :, g * LANES:(g + 1) * LANES] - m_b)
                             for g in range(tq // LANES)], axis=-1)
        l_scr[...] += _lane_fold(p, jnp.add)
        acc_scr[...] += _dot(p.astype(_bf16), v_ref[pl.ds(off, tq), :])
        return carry

    lax.fori_loop(0, qi + 1, value_pass, 0)
    o_ref[...] = (acc_scr[...] / jnp.sum(l_scr[...], axis=-1, keepdims=True)).astype(o_ref.dtype)


def _attention(q, k, v):
    B, S, _ = q.shape
    nq = S // TQ
    return pl.pallas_call(
        _attn_kernel,
        grid=(B, N_HEADS, nq),
        in_specs=[pl.BlockSpec((None, TQ, HEAD_PAD), lambda b, h, i: (b, i, h)),
                  pl.BlockSpec((None, S, HEAD_PAD), lambda b, h, i: (b, 0, h)),
                  pl.BlockSpec((None, S, V_DIM), lambda b, h, i: (b, 0, h))],
        out_specs=pl.BlockSpec((None, TQ, V_DIM), lambda b, h, i: (b, i, h)),
        out_shape=jax.ShapeDtypeStruct((B, S, ATTN_W), _bf16),
        scratch_shapes=[pltpu.VMEM((nq, TQ, TQ), _f32), pltpu.VMEM((TQ, LANES), _f32),
                        pltpu.VMEM((TQ, LANES), _f32), pltpu.VMEM((TQ, V_DIM), _f32)],
        compiler_params=pltpu.CompilerParams(dimension_semantics=("arbitrary", "arbitrary", "arbitrary"),
                                             vmem_limit_bytes=VMEM_LIMIT),
        name="attention",
    )(q, k, v)


def _post_kernel(x_ref, mix_ref, att_ref, aog_ref, wo_ref, fng_ref, wr_ref, br_ref,
                 x1_ref, h2_ref, route_ref, rgate_ref, cnt_ref):
    tm = x_ref.shape[0]
    att = att_ref[...].astype(_f32)
    att_n = (att * _rms(att, ATTN_W) * aog_ref[...]).astype(_bf16)
    x1 = x_ref[...] + _dot(mix_ref[...], wo_ref[0:POOL_W + CONV_W, :]) + _dot(att_n, wo_ref[POOL_W + CONV_W:, :])
    x1_ref[...] = x1
    h2 = (x1 * _rms(x1, D_MODEL) * fng_ref[...]).astype(_bf16)
    h2_ref[...] = h2

    logits = _dot(h2, wr_ref[...]) + br_ref[...]
    lane = lax.broadcasted_iota(jnp.int32, logits.shape, 1)
    lane_f = lane.astype(_f32)
    work = logits
    sel = jnp.zeros(logits.shape, _f32)
    onehots, vals, idxs = [], [], []
    for _ in range(TOP_K):
        mx = jnp.max(work, axis=-1, keepdims=True)
        idx_f = jnp.min(jnp.where(work == mx, lane_f, float(LANES)), axis=-1, keepdims=True)
        idx = idx_f.astype(jnp.int32)
        hot = lane == idx
        onehots.append(hot)
        vals.append(mx)
        idxs.append(idx)
        sel = jnp.where(hot, 1.0, sel)
        work = jnp.where(hot, -jnp.inf, work)
    exps = [jnp.exp(vk - vals[0]) for vk in vals]
    denom = exps[0] + exps[1] + exps[2] + exps[3]

    nt = tm // ROUTE_TILE
    r_i = lax.broadcasted_iota(jnp.int32, (ROUTE_TILE, ROUTE_TILE), 0)
    c_i = lax.broadcasted_iota(jnp.int32, (ROUTE_TILE, ROUTE_TILE), 1)
    tri = jnp.where(c_i < r_i, 1.0, 0.0).astype(_bf16)
    upper = jnp.where(r_i < c_i, 1.0, 0.0).astype(_bf16)
    sel_b = sel.astype(_bf16)
    ranks, cnts = [], []
    for s in range(nt):
        rows = slice(s * ROUTE_TILE, (s + 1) * ROUTE_TILE)
        ranks.append(_dot(tri, sel_b[rows, :]))
        cnts.append(jnp.sum(sel[rows, :], axis=0, keepdims=True))
    cnt = jnp.concatenate(cnts, axis=0)
    cnt_ref[...] = cnt
    chunks = jnp.floor((cnt + (ROW_CHUNK - 1)) * (1.0 / ROW_CHUNK))
    boff = _dot(chunks.astype(_bf16), upper) * float(ROW_CHUNK)
    slot = jnp.concatenate(
        [ranks[s] + jnp.broadcast_to(boff[s:s + 1, :], (ROUTE_TILE, LANES)) for s in range(nt)], axis=0)

    route = jnp.zeros(logits.shape, jnp.int32)
    rgate = jnp.zeros(logits.shape, _f32)
    for kk in range(TOP_K):
        col = jnp.sum(jnp.where(onehots[kk], slot, 0.0), axis=-1, keepdims=True).astype(jnp.int32)
        route = jnp.where(lane == kk, idxs[kk], route)
        route = jnp.where(lane == TOP_K + kk, col, route)
        rgate = jnp.where(lane == kk, exps[kk] / denom, rgate)
    route_ref[...] = route
    rgate_ref[...] = rgate


def _post_mix(layer, x, mix, att, p):
    T = x.shape[0]
    tm = TM_POST
    row = lambda w: pl.BlockSpec((tm, w), lambda i: (i, 0))
    names = ("attn_out_g", "w_o", "ffn_norm_g", "w_router", "b_router")
    weights = [p[n] for n in names]
    out_shape = (jax.ShapeDtypeStruct((T, D_MODEL), _f32), jax.ShapeDtypeStruct((T, D_MODEL), _bf16),
                 jax.ShapeDtypeStruct((T, LANES), jnp.int32), jax.ShapeDtypeStruct((T, LANES), _f32),
                 jax.ShapeDtypeStruct((T // ROUTE_TILE, LANES), _f32))
    return pl.pallas_call(
        _post_kernel,
        grid=(T // tm,),
        in_specs=[row(D_MODEL), row(POOL_W + CONV_W), row(ATTN_W)] + [_layer_spec(w.shape, layer, 1) for w in weights],
        out_specs=(row(D_MODEL), row(D_MODEL), row(LANES), row(LANES),
                   pl.BlockSpec((tm // ROUTE_TILE, LANES), lambda i: (i, 0))),
        out_shape=out_shape,
        compiler_params=pltpu.CompilerParams(dimension_semantics=("arbitrary",), vmem_limit_bytes=VMEM_LIMIT),
        name="post_mix",
    )(x, mix, att, *weights)


N_SLAB = D_MODEL // LANES
assert N_SLAB == SUBLANES
CHUNK_SUBROWS = ROW_CHUNK * N_SLAB


def _to_token_major(ref, value):
    rows = value.shape[0]
    for j in range(N_SLAB):
        ref[pl.ds(j, rows, stride=N_SLAB), :] = value[:, j * LANES:(j + 1) * LANES]


def _from_token_major(ref, rows):
    return jnp.concatenate([ref[pl.ds(j, rows, stride=N_SLAB), :] for j in range(N_SLAB)], axis=-1)


def _subrow(row):
    return pl.multiple_of(row * N_SLAB, N_SLAB)


MAX_TILE_CHUNKS = ROUTE_TILE * TOP_K // ROW_CHUNK + N_EXPERTS


def _chunk_loop(chunk_dst, n_total, tile, copy_fn):
    def body(c, carry):
        copy_fn(c * ROW_CHUNK, chunk_dst[tile * MAX_TILE_CHUNKS + c])
        return carry

    lax.fori_loop(0, n_total[tile], body, 0)


def _dispatch_kernel(chunk_dst, n_total, pad_start, pad_chunks,
                     slot_ref, h2_ref, xs_hbm, stage, zeros, sems, zsem):
    i = pl.program_id(0)
    n_steps = pl.num_programs(0)
    cur = i % 2

    def chunk_copy(slot, stage_row, sorted_row):
        return pltpu.make_async_copy(stage.at[slot, pl.ds(_subrow(stage_row), CHUNK_SUBROWS)],
                                     xs_hbm.at[pl.ds(_subrow(sorted_row), CHUNK_SUBROWS)], sems.at[slot])

    def wait_tile(tile, slot):
        def body(c, carry):
            chunk_copy(slot, 0, 0).wait()
            return carry
        lax.fori_loop(0, n_total[tile], body, 0)

    @pl.when(i == 0)
    def _():
        zeros[...] = jnp.zeros(zeros.shape, _f32)

        def zero_copy(row):
            return pltpu.make_async_copy(zeros, xs_hbm.at[pl.ds(_subrow(row), CHUNK_SUBROWS)], zsem)

        for e in range(N_EXPERTS + 1):
            def start(c, carry, e=e):
                zero_copy(pad_start[e] + c * ROW_CHUNK).start()
                return carry
            lax.fori_loop(0, pad_chunks[e], start, 0)
        for e in range(N_EXPERTS + 1):
            def wait(c, carry, e=e):
                zero_copy(pad_start[e] + c * ROW_CHUNK).wait()
                return carry
            lax.fori_loop(0, pad_chunks[e], wait, 0)

    slots = slot_ref[...]
    row_i = lax.broadcasted_iota(jnp.int32, (STAGE_ROWS, ROUTE_TILE), 0)
    place = jnp.zeros((STAGE_ROWS, ROUTE_TILE), _f32)
    for kk in range(TOP_K):
        place = place + jnp.where(row_i == slots[kk:kk + 1, :], 1.0, 0.0)
    staged = _dot(place.astype(_bf16), h2_ref[...])
    _to_token_major(stage.at[cur], staged)

    @pl.when(i > 0)
    def _():
        wait_tile(i - 1, 1 - cur)

    _chunk_loop(chunk_dst, n_total, i, lambda sr, dr: chunk_copy(cur, sr, dr).start())

    @pl.when(i == n_steps - 1)
    def _():
        wait_tile(i, cur)


def _dispatch(h2, slots_t, sched, n_rows):
    T = h2.shape[0]
    grid_spec = pltpu.PrefetchScalarGridSpec(
        num_scalar_prefetch=4,
        grid=(T // ROUTE_TILE,),
        in_specs=[pl.BlockSpec((None, SUBLANES, ROUTE_TILE), lambda i, *_: (i, 0, 0)),
                  pl.BlockSpec((ROUTE_TILE, D_MODEL), lambda i, *_: (i, 0))],
        out_specs=pl.BlockSpec(memory_space=pl.ANY),
        scratch_shapes=[pltpu.VMEM((2, STAGE_ROWS * N_SLAB, LANES), _f32),
                        pltpu.VMEM((CHUNK_SUBROWS, LANES), _f32),
                        pltpu.SemaphoreType.DMA((2,)), pltpu.SemaphoreType.DMA(())],
    )
    return pl.pallas_call(
        _dispatch_kernel,
        grid_spec=grid_spec,
        out_shape=jax.ShapeDtypeStruct((n_rows * N_SLAB, LANES), _f32),
        compiler_params=pltpu.CompilerParams(dimension_semantics=("arbitrary",), vmem_limit_bytes=VMEM_LIMIT),
        name="dispatch",
    )(sched["chunk_dst"], sched["n_total"], sched["pad_start"], sched["pad_chunks"], slots_t, h2)


def _expert_kernel(blk_exp, blk_valid, xs_ref, wgu_ref, bgu_ref, wd_ref, bd_ref, y_ref):
    i = pl.program_id(0)
    valid = blk_valid[i]

    @pl.when(valid > 0)
    def _():
        x = _from_token_major(xs_ref, EXPERT_BLOCK).astype(_bf16)
        y = jnp.zeros((EXPERT_BLOCK, D_MODEL), _f32)
        for c in range(D_FF // FF_CHUNK):
            lo = c * FF_CHUNK
            hg = _dot(x, wgu_ref[:, lo:lo + FF_CHUNK]) + bgu_ref[:, lo:lo + FF_CHUNK]
            hl = _dot(x, wgu_ref[:, D_FF + lo:D_FF + lo + FF_CHUNK]) + bgu_ref[:, D_FF + lo:D_FF + lo + FF_CHUNK]
            g = jnp.minimum(hg, SWIGLU_LIMIT)
            lin = jnp.clip(hl, -SWIGLU_LIMIT, SWIGLU_LIMIT)
            act = g * jax.nn.sigmoid(SWIGLU_ALPHA * g) * (lin + 1.0)
            y = y + _dot(act.astype(_bf16), wd_ref[lo:lo + FF_CHUNK, :])
        _to_token_major(y_ref, y + bd_ref[...])

    @pl.when(valid == 0)
    def _():
        y_ref[...] = jnp.zeros(y_ref.shape, _f32)


def _experts(layer, xs, blk_exp, blk_valid, w_gu, b_gu, w_down, b_down):
    n_rows = xs.shape[0] // N_SLAB
    nb = n_rows // EXPERT_BLOCK
    grid_spec = pltpu.PrefetchScalarGridSpec(
        num_scalar_prefetch=2,
        grid=(nb,),
        in_specs=[pl.BlockSpec((EXPERT_BLOCK * N_SLAB, LANES), lambda i, be, bv: (i, 0)),
                  pl.BlockSpec((None, None, D_MODEL, 2 * D_FF), lambda i, be, bv: (layer, be[i], 0, 0)),
                  pl.BlockSpec((None, None, 1, 2 * D_FF), lambda i, be, bv: (layer, be[i], 0, 0)),
                  pl.BlockSpec((None, None, D_FF, D_MODEL), lambda i, be, bv: (layer, be[i], 0, 0)),
                  pl.BlockSpec((None, None, 1, D_MODEL), lambda i, be, bv: (layer, be[i], 0, 0))],
        out_specs=pl.BlockSpec((EXPERT_BLOCK * N_SLAB, LANES), lambda i, be, bv: (i, 0)),
    )
    return pl.pallas_call(
        _expert_kernel,
        grid_spec=grid_spec,
        out_shape=jax.ShapeDtypeStruct((n_rows * N_SLAB, LANES), _f32),
        compiler_params=pltpu.CompilerParams(dimension_semantics=("arbitrary",), vmem_limit_bytes=VMEM_LIMIT),
        name="experts",
    )(blk_exp, blk_valid, xs, w_gu, b_gu, w_down, b_down)


def _combine_kernel(chunk_dst, n_total, x1_ref, route_ref, gate_ref, ys_hbm, out_ref, stage, sems):
    i = pl.program_id(0)
    n_steps = pl.num_programs(0)
    cur = i % 2

    def chunk_copy(slot, stage_row, sorted_row):
        return pltpu.make_async_copy(ys_hbm.at[pl.ds(_subrow(sorted_row), CHUNK_SUBROWS)],
                                     stage.at[slot, pl.ds(_subrow(stage_row), CHUNK_SUBROWS)], sems.at[slot])

    def fetch_tile(tile, slot):
        _chunk_loop(chunk_dst, n_total, tile, lambda sr, dr: chunk_copy(slot, sr, dr).start())

    @pl.when(i == 0)
    def _():
        stage[...] = jnp.zeros(stage.shape, _f32)
        fetch_tile(0, 0)

    @pl.when(i + 1 < n_steps)
    def _():
        fetch_tile(i + 1, 1 - cur)

    def wait_one(c, carry):
        chunk_copy(cur, 0, 0).wait()
        return carry

    lax.fori_loop(0, n_total[i], wait_one, 0)

    rows = _from_token_major(stage.at[cur], STAGE_ROWS).astype(_bf16)
    route = route_ref[...]
    gates = gate_ref[...]
    lane_i = lax.broadcasted_iota(jnp.int32, (ROUTE_TILE, STAGE_ROWS), 1)
    weight = jnp.zeros((ROUTE_TILE, STAGE_ROWS), _f32)
    for kk in range(TOP_K):
        weight = weight + jnp.where(lane_i == route[:, TOP_K + kk:TOP_K + kk + 1], gates[:, kk:kk + 1], 0.0)
    out_ref[...] = x1_ref[...] + _dot(weight.astype(_bf16), rows)


def _combine(x1, route, rgate, ys, sched):
    T = x1.shape[0]
    tc = ROUTE_TILE
    grid_spec = pltpu.PrefetchScalarGridSpec(
        num_scalar_prefetch=2,
        grid=(T // tc,),
        in_specs=[pl.BlockSpec((tc, D_MODEL), lambda i, *_: (i, 0)),
                  pl.BlockSpec((tc, LANES), lambda i, *_: (i, 0)),
                  pl.BlockSpec((tc, LANES), lambda i, *_: (i, 0)),
                  pl.BlockSpec(memory_space=pl.ANY)],
        out_specs=pl.BlockSpec((tc, D_MODEL), lambda i, *_: (i, 0)),
        scratch_shapes=[pltpu.VMEM((2, STAGE_ROWS * N_SLAB, LANES), _f32),
                        pltpu.SemaphoreType.DMA((2,))],
    )
    return pl.pallas_call(
        _combine_kernel,
        grid_spec=grid_spec,
        out_shape=jax.ShapeDtypeStruct((T, D_MODEL), _f32),
        compiler_params=pltpu.CompilerParams(dimension_semantics=("arbitrary",), vmem_limit_bytes=VMEM_LIMIT),
        name="combine",
    )(sched["chunk_dst"], sched["n_total"], x1, route, rgate, ys)


def _prepare_params(w):
    L = w["w_in"].shape[0]
    p = {}
    row = lambda a: a.reshape(L, 1, a.shape[-1])
    p["attn_norm_g"] = row(w["attn_norm_g"])
    p["w_in"] = jnp.pad(w["w_in"], ((0, 0), (0, 0), (0, D_IN_PAD - w["w_in"].shape[-1]))).astype(_bf16)
    eye = jnp.eye(len(POOL_WINDOWS), dtype=_f32)
    p["pool_w"] = jnp.einsum("lgcd,gh->lgchd", w["pool_w"], eye).reshape(L, POOL_W, POOL_W).astype(_bf16)
    p["pool_scale"] = row(w["pool_scale"])
    p["conv_dw"] = w["conv_dw"]
    p["conv_dw_b"] = row(w["conv_dw_b"])
    p["conv_ln_g"] = row(w["conv_ln_g"])
    p["conv_ln_b"] = row(w["conv_ln_b"])
    p["conv_pw"] = w["conv_pw"].astype(_bf16)
    p["conv_pw_b"] = row(w["conv_pw_b"])
    p["q_a_g"] = row(w["q_a_g"])
    wuq = w["w_uq"].reshape(L, Q_LORA, N_HEADS, QK_HEAD)
    p["w_uq"] = jnp.pad(wuq, ((0, 0), (0, 0), (0, 0), (0, HEAD_PAD - QK_HEAD))).reshape(
        L, Q_LORA, N_HEADS * HEAD_PAD).astype(_bf16)
    p["kv_a_g"] = row(w["kv_a_g"])
    wukv = w["w_ukv"].reshape(L, KV_LORA, N_HEADS, QK_NOPE + V_DIM)
    p["w_ukv"] = jnp.concatenate([wukv[..., :QK_NOPE].reshape(L, KV_LORA, N_HEADS * QK_NOPE),
                                  wukv[..., QK_NOPE:].reshape(L, KV_LORA, N_HEADS * V_DIM)], axis=-1).astype(_bf16)
    pad_head = lambda g: jnp.pad(g, ((0, 0), (0, HEAD_PAD - QK_HEAD))).reshape(L, 1, HEAD_PAD)
    p["q_norm_g"] = pad_head(w["q_norm_g"])
    p["k_norm_g"] = pad_head(w["k_norm_g"])
    p["pool_out_g"] = row(w["pool_out_g"])
    p["conv_out_g"] = row(w["conv_out_g"])
    p["attn_out_g"] = row(w["attn_out_g"])
    p["w_o"] = w["w_o"].astype(_bf16)
    p["ffn_norm_g"] = row(w["ffn_norm_g"])
    p["w_router"] = jnp.pad(w["w_router"], ((0, 0), (0, 0), (0, LANES - N_EXPERTS))).astype(_bf16)
    p["b_router"] = jnp.pad(w["b_router"], ((0, 0), (0, LANES - N_EXPERTS)),
                            constant_values=NEG_BIG).reshape(L, 1, LANES)
    E = w["w_gu"].shape[1]
    wgu = w["w_gu"].reshape(L, E, D_MODEL, D_FF, 2)
    p["w_gu"] = jnp.concatenate([wgu[..., 0], wgu[..., 1]], axis=-1).astype(_bf16)
    bgu = w["b_gu"].reshape(L, E, D_FF, 2)
    p["b_gu"] = jnp.concatenate([bgu[..., 0], bgu[..., 1]], axis=-1).reshape(L, E, 1, 2 * D_FF)
    p["w_down"] = w["w_down"].astype(_bf16)
    p["b_down"] = w["b_down"].reshape(L, E, 1, D_MODEL)
    return p


def _rope_tables(positions):
    inv_freq = 1.0 / (ROPE_THETA ** (jnp.arange(0, QK_ROPE, 2, dtype=_f32) / QK_ROPE))
    ang = positions.astype(_f32)[..., None] * inv_freq
    cos, sin = jnp.cos(ang), jnp.sin(ang)
    zeros = jnp.zeros(cos.shape[:-1] + (LANES - QK_ROPE,), _f32)
    return (jnp.concatenate([cos, cos, zeros], axis=-1), jnp.concatenate([-sin, sin, zeros], axis=-1))


def _routing_schedule(cnt_f, n_tokens):
    n = cnt_f[:, 0:N_EXPERTS].astype(jnp.int32)
    counts = jnp.sum(n, axis=0)
    spare = ROW_CHUNK - 1
    padded = jnp.where(counts > 0, (counts + spare + EXPERT_BLOCK - 1) // EXPERT_BLOCK * EXPERT_BLOCK, 0)
    pend = jnp.cumsum(padded)
    pstart = pend - padded
    base = pstart[None, :] + jnp.cumsum(n, axis=0) - n
    n_chunks = (n + ROW_CHUNK - 1) // ROW_CHUNK
    max_rows = n_tokens * TOP_K + N_EXPERTS * (spare + EXPERT_BLOCK - 1) + ROW_CHUNK
    n_blocks = -(-max_rows // EXPERT_BLOCK)
    blk_row = jnp.arange(n_blocks, dtype=jnp.int32) * EXPERT_BLOCK
    blk_exp = jnp.minimum(jnp.sum((blk_row[:, None] >= pend[None, :]).astype(jnp.int32), axis=1), N_EXPERTS - 1)
    onehot = (blk_exp[:, None] == jnp.arange(N_EXPERTS, dtype=jnp.int32)[None, :]).astype(jnp.int32)
    seg_end = jnp.sum(onehot * (pstart + counts)[None, :], axis=1)
    blk_valid = jnp.where(blk_row < pend[-1], jnp.clip(seg_end - blk_row, 0, EXPERT_BLOCK), 0)
    i32 = lambda a: a.astype(jnp.int32)
    n_rows = n_blocks * EXPERT_BLOCK
    seg_chunks = (padded - counts + ROW_CHUNK - 1) // ROW_CHUNK
    pad_start = jnp.concatenate([pend - seg_chunks * ROW_CHUNK, pend[-1:]])
    pad_chunks = jnp.concatenate([seg_chunks, (n_rows - pend[-1:]) // ROW_CHUNK])
    cum_end = jnp.cumsum(n_chunks, axis=1)
    cum_start = cum_end - n_chunks
    c_idx = jnp.arange(MAX_TILE_CHUNKS, dtype=jnp.int32)
    e_of = jnp.minimum(jnp.sum((c_idx[None, :, None] >= cum_end[:, None, :]).astype(jnp.int32), axis=-1),
                       N_EXPERTS - 1)
    pick = (e_of[:, :, None] == jnp.arange(N_EXPERTS, dtype=jnp.int32)[None, None, :]).astype(jnp.int32)
    chunk_dst = (jnp.sum(pick * base[:, None, :], axis=-1)
                 + (c_idx[None, :] - jnp.sum(pick * cum_start[:, None, :], axis=-1)) * ROW_CHUNK)
    sched = dict(chunk_dst=i32(chunk_dst).reshape(-1), n_total=i32(cum_end[:, -1]),
                 pad_start=i32(pad_start), pad_chunks=i32(pad_chunks))
    return sched, i32(blk_exp), i32(blk_valid), n_rows


def kernel(x, positions, attn_norm_g, w_in, pool_w, pool_scale, conv_dw, conv_dw_b, conv_ln_g, conv_ln_b, conv_pw, conv_pw_b, q_a_g, w_uq, kv_a_g, w_ukv, q_norm_g, k_norm_g, pool_out_g, conv_out_g, attn_out_g, w_o, ffn_norm_g, w_router, b_router, w_gu, b_gu, w_down, b_down):
    B, S, D = x.shape
    T = B * S
    depth = w_in.shape[0]
    p = _prepare_params(dict(
        attn_norm_g=attn_norm_g, w_in=w_in, pool_w=pool_w, pool_scale=pool_scale, conv_dw=conv_dw,
        conv_dw_b=conv_dw_b, conv_ln_g=conv_ln_g, conv_ln_b=conv_ln_b, conv_pw=conv_pw, conv_pw_b=conv_pw_b,
        q_a_g=q_a_g, w_uq=w_uq, kv_a_g=kv_a_g, w_ukv=w_ukv, q_norm_g=q_norm_g, k_norm_g=k_norm_g,
        pool_out_g=pool_out_g, conv_out_g=conv_out_g, attn_out_g=attn_out_g, w_o=w_o, ffn_norm_g=ffn_norm_g,
        w_router=w_router, b_router=b_router, w_gu=w_gu, b_gu=b_gu, w_down=w_down, b_down=b_down))
    rope_c, rope_s = _rope_tables(positions)
    for layer in range(depth):
        mix, q, k, v = _mixer_front(layer, x, rope_c, rope_s, p)
        att = _attention(q, k, v)
        x1, h2, route, rgate, cnt = _post_mix(layer, x.reshape(T, D), mix.reshape(T, -1), att.reshape(T, -1), p)
        sched, blk_exp, blk_valid, n_rows = _routing_schedule(cnt, T)
        slots_t = route[:, TOP_K:2 * TOP_K].reshape(T // ROUTE_TILE, ROUTE_TILE, TOP_K).transpose(0, 2, 1)
        slots_t = jnp.pad(slots_t, ((0, 0), (0, SUBLANES - TOP_K), (0, 0)), constant_values=-1)
        xs = _dispatch(h2, slots_t, sched, n_rows)
        ys = _experts(layer, xs, blk_exp, blk_valid, p["w_gu"], p["b_gu"], p["w_down"], p["b_down"])
        x = _combine(x1, route, rgate, ys, sched).reshape(B, S, D)
    return x
```

```python
import functools
import math

import jax
import jax.numpy as jnp
from jax import lax
from jax.experimental import pallas as pl
from jax.experimental.pallas import tpu as pltpu

D_MODEL = 1024
CHUNK = 64
POOL_W = 256
POOL_WINDOWS = (2, 4, 8, 16)
POOL_GC = 64
CONV_W = 256
CONV_K = 31
N_HEADS = 4
QK_NOPE = 128
QK_ROPE = 64
QK_HEAD = QK_NOPE + QK_ROPE
V_DIM = 128
Q_LORA = 256
KV_LORA = 128
ATTN_W = N_HEADS * V_DIM
ROPE_THETA = 10000.0
N_EXPERTS = 32
TOP_K = 4
D_FF = D_MODEL
SWIGLU_LIMIT = 7.0
SWIGLU_ALPHA = 1.702
EPS = 1e-6

LANES = 128
HEAD_PAD = 2 * LANES
D_IN_PAD = 1280
POOL_HALO = 16
CONV_HALO = 32
NEG_BIG = -1e30

TM_FRONT = 512
TQ = 512
TM_POST = 1024
ROUTE_TILE = 128
ROW_CHUNK = 8
SUBLANES = 8
STAGE_ROWS = -(-(ROUTE_TILE * TOP_K + N_EXPERTS * (ROW_CHUNK - 1)) // LANES) * LANES
EXPERT_BLOCK = 512
FF_CHUNK = 256
VMEM_LIMIT = 56 * 1024 * 1024

_bf16 = jnp.bfloat16
_f32 = jnp.float32


def _dot(a, b):
    return jnp.dot(a, b, preferred_element_type=_f32)


def _rms(x, width):
    return lax.rsqrt(jnp.sum(x * x, axis=-1, keepdims=True) * (1.0 / width) + EPS)


def _layer_spec(shape, layer, n_grid):
    nd = len(shape)
    block = (None,) + tuple(shape[1:])
    if n_grid == 1:
        return pl.BlockSpec(block, lambda i: (layer,) + (0,) * (nd - 1))
    return pl.BlockSpec(block, lambda i, j: (layer,) + (0,) * (nd - 1))


def _rope(x, c_tab, s_tab):
    lane = lax.broadcasted_iota(jnp.int32, x.shape, 1)
    partner = jnp.where(lane < QK_ROPE // 2, pltpu.roll(x, LANES - QK_ROPE // 2, 1),
                        pltpu.roll(x, QK_ROPE // 2, 1))
    return x * c_tab + partner * s_tab


def _front_kernel(x_ref, ropec_ref, ropes_ref, ang_ref, win_ref, poolw_ref, pscale_ref, cdw_ref, cdwb_ref,
                  clng_ref, clnb_ref, cpw_ref, cpwb_ref, qag_ref, wuq_ref, kvag_ref, wukv_ref, qng_ref,
                  kng_ref, poutg_ref, coutg_ref,
                  mix_ref, q_ref, k_ref, v_ref, pool_ext, conv_ext):
    j = pl.program_id(1)
    tm = x_ref.shape[0]

    x = x_ref[...]
    h = x * _rms(x, D_MODEL) * ang_ref[...]
    z = _dot(h.astype(_bf16), win_ref[...])

    u = z[:, 0:POOL_W]

    @pl.when(j == 0)
    def _():
        pool_ext[0:POOL_HALO, :] = jnp.zeros((POOL_HALO, POOL_W), _f32)
        conv_ext[0:CONV_HALO, :] = jnp.zeros((CONV_HALO, CONV_W), _f32)

    @pl.when(j > 0)
    def _():
        pool_ext[0:POOL_HALO, :] = pool_ext[tm:tm + POOL_HALO, :]
        conv_ext[0:CONV_HALO, :] = conv_ext[tm:tm + CONV_HALO, :]

    pool_ext[POOL_HALO:POOL_HALO + tm, :] = u
    t_pos = (j * tm + lax.broadcasted_iota(jnp.int32, (tm, LANES), 0) + 1).astype(_f32)
    lane = lax.broadcasted_iota(jnp.int32, (tm, LANES), 1)
    pooled_halves = []
    for half, (w_lo, w_hi) in enumerate(((POOL_WINDOWS[0], POOL_WINDOWS[1]),
                                         (POOL_WINDOWS[2], POOL_WINDOWS[3]))):
        cols = slice(half * LANES, (half + 1) * LANES)
        s_lo = pool_ext[POOL_HALO:POOL_HALO + tm, cols]
        for d in range(1, w_lo):
            s_lo = s_lo + pool_ext[POOL_HALO - d:POOL_HALO - d + tm, cols]
        s_hi = s_lo
        for d in range(w_lo, w_hi):
            s_hi = s_hi + pool_ext[POOL_HALO - d:POOL_HALO - d + tm, cols]
        first = lane < POOL_GC
        win_sum = jnp.where(first, s_lo, s_hi)
        cnt = jnp.minimum(t_pos, jnp.where(first, float(w_lo), float(w_hi)))
        pooled_halves.append(win_sum / cnt - u[:, cols])
    pooled = jnp.concatenate(pooled_halves, axis=-1)
    y_pool = _dot(pooled.astype(_bf16), poolw_ref[...]) * pscale_ref[...]
    y_pool = y_pool * _rms(y_pool, POOL_W) * poutg_ref[...]

    a = z[:, POOL_W:POOL_W + CONV_W]
    gate = z[:, POOL_W + CONV_W:POOL_W + 2 * CONV_W]
    conv_ext[CONV_HALO:CONV_HALO + tm, :] = a * jax.nn.sigmoid(gate)
    base = CONV_HALO - (CONV_K - 1)
    acc = cdwb_ref[...]
    window = conv_ext[...]
    n_win = window.shape[0]
    for r in range(SUBLANES):
        tiles = [(base + kk) // SUBLANES for kk in range(CONV_K) if (base + kk) % SUBLANES == r]
        shifted = window if r == 0 else pltpu.roll(window, n_win - r, 0)
        for m in tiles:
            kk = m * SUBLANES + r - base
            acc = acc + shifted[m * SUBLANES:m * SUBLANES + tm, :] * cdw_ref[kk:kk + 1, :]
    mu = jnp.mean(acc, axis=-1, keepdims=True)
    cen = acc - mu
    var = jnp.mean(cen * cen, axis=-1, keepdims=True)
    ln = cen * lax.rsqrt(var + EPS) * clng_ref[...] + clnb_ref[...]
    sw = ln * jax.nn.sigmoid(ln)
    y_conv = _dot(sw.astype(_bf16), cpw_ref[...]) + cpwb_ref[...]
    y_conv = y_conv * _rms(y_conv, CONV_W) * coutg_ref[...]

    mix_ref[:, 0:POOL_W] = y_pool.astype(_bf16)
    mix_ref[:, POOL_W:POOL_W + CONV_W] = y_conv.astype(_bf16)

    s1 = POOL_W + 2 * CONV_W
    c_q = z[:, s1:s1 + Q_LORA]
    c_kv = z[:, s1 + Q_LORA:s1 + Q_LORA + KV_LORA]
    k_rope = z[:, s1 + Q_LORA + KV_LORA:s1 + Q_LORA + KV_LORA + LANES]
    c_tab = ropec_ref[...]
    s_tab = ropes_ref[...]

    q_all = _dot((c_q * _rms(c_q, Q_LORA) * qag_ref[...]).astype(_bf16), wuq_ref[...])
    kv_all = _dot((c_kv * _rms(c_kv, KV_LORA) * kvag_ref[...]).astype(_bf16), wukv_ref[...])
    sm_scale = 1.0 / math.sqrt(QK_HEAD)
    kr_ss = jnp.sum(k_rope * k_rope, axis=-1, keepdims=True)
    kr_rot = _rope(k_rope * kng_ref[:, LANES:2 * LANES], c_tab, s_tab)
    for hd in range(N_HEADS):
        qh = q_all[:, hd * HEAD_PAD:(hd + 1) * HEAD_PAD]
        qh = qh * (_rms(qh, QK_HEAD) * sm_scale) * qng_ref[...]
        q_ref[:, hd * HEAD_PAD:hd * HEAD_PAD + LANES] = qh[:, 0:LANES].astype(_bf16)
        q_ref[:, hd * HEAD_PAD + LANES:(hd + 1) * HEAD_PAD] = _rope(qh[:, LANES:], c_tab, s_tab).astype(_bf16)
        kn = kv_all[:, hd * QK_NOPE:(hd + 1) * QK_NOPE]
        k_rs = lax.rsqrt((jnp.sum(kn * kn, axis=-1, keepdims=True) + kr_ss) * (1.0 / QK_HEAD) + EPS)
        k_ref[:, hd * HEAD_PAD:hd * HEAD_PAD + LANES] = (kn * k_rs * kng_ref[:, 0:LANES]).astype(_bf16)
        k_ref[:, hd * HEAD_PAD + LANES:(hd + 1) * HEAD_PAD] = (kr_rot * k_rs).astype(_bf16)
    v_ref[...] = kv_all[:, N_HEADS * QK_NOPE:].astype(_bf16)


def _mixer_front(layer, x, rope_c, rope_s, p):
    B, S, _ = x.shape
    tm = TM_FRONT
    row = lambda w: pl.BlockSpec((None, tm, w), lambda b, j: (b, j, 0))
    names = ("attn_norm_g", "w_in", "pool_w", "pool_scale", "conv_dw", "conv_dw_b", "conv_ln_g", "conv_ln_b",
             "conv_pw", "conv_pw_b", "q_a_g", "w_uq", "kv_a_g", "w_ukv", "q_norm_g", "k_norm_g",
             "pool_out_g", "conv_out_g")
    weights = [p[n] for n in names]
    in_specs = [row(D_MODEL), row(LANES), row(LANES)] + [_layer_spec(w.shape, layer, 2) for w in weights]
    out_shape = (jax.ShapeDtypeStruct((B, S, POOL_W + CONV_W), _bf16),
                 jax.ShapeDtypeStruct((B, S, N_HEADS * HEAD_PAD), _bf16),
                 jax.ShapeDtypeStruct((B, S, N_HEADS * HEAD_PAD), _bf16),
                 jax.ShapeDtypeStruct((B, S, ATTN_W), _bf16))
    out_specs = (row(POOL_W + CONV_W), row(N_HEADS * HEAD_PAD), row(N_HEADS * HEAD_PAD), row(ATTN_W))
    return pl.pallas_call(
        _front_kernel,
        grid=(B, S // tm),
        in_specs=in_specs,
        out_specs=out_specs,
        out_shape=out_shape,
        scratch_shapes=[pltpu.VMEM((tm + POOL_HALO, POOL_W), _f32),
                        pltpu.VMEM((tm + CONV_HALO, CONV_W), _f32)],
        compiler_params=pltpu.CompilerParams(dimension_semantics=("arbitrary", "arbitrary"),
                                             vmem_limit_bytes=VMEM_LIMIT),
        name="mixer_front",
    )(x, rope_c, rope_s, *weights)


def _lane_fold(x, op):
    out = x[:, 0:LANES]
    for g in range(1, x.shape[1] // LANES):
        out = op(out, x[:, g * LANES:(g + 1) * LANES])
    return out


def _attn_kernel(q_ref, k_ref, v_ref, o_ref, s_scr, m_scr, l_scr, acc_scr):
    qi = pl.program_id(2)
    q = q_ref[...]
    tq = q.shape[0]

    def scores(c):
        off = pl.multiple_of(c * tq, tq)
        return lax.dot_general(q, k_ref[pl.ds(off, tq), :], (((1,), (1,)), ((), ())),
                               preferred_element_type=_f32)

    m_scr[...] = jnp.full(m_scr.shape, NEG_BIG, _f32)

    def score_pass(c, carry):
        s = scores(c)
        s_scr[c] = s
        m_scr[...] = jnp.maximum(m_scr[...], _lane_fold(s, jnp.maximum))
        return carry

    lax.fori_loop(0, qi, score_pass, 0)
    s = scores(qi)
    q_chunk = lax.broadcasted_iota(jnp.int32, s.shape, 0) // CHUNK
    k_chunk = lax.broadcasted_iota(jnp.int32, s.shape, 1) // CHUNK
    s = jnp.where(k_chunk <= q_chunk, s, NEG_BIG)
    s_scr[qi] = s
    m_row = jnp.max(jnp.maximum(m_scr[...], _lane_fold(s, jnp.maximum)), axis=-1, keepdims=True)
    m_scr[...] = jnp.broadcast_to(m_row, m_scr.shape)
    l_scr[...] = jnp.zeros(l_scr.shape, _f32)
    acc_scr[...] = jnp.zeros(acc_scr.shape, _f32)

    def value_pass(c, carry):
        off = pl.multiple_of(c * tq, tq)
        m_b = m_scr[...]
        p = jnp.concatenate([jnp.exp(s_scr[c, :, g * LANES:(g + 1) * LANES] - m_b)
                             for g in range(tq // LANES)], axis=-1)
        l_scr[...] += _lane_fold(p, jnp.add)
        acc_scr[...] += _dot(p.astype(_bf16), v_ref[pl.ds(off, tq), :])
        return carry

    lax.fori_loop(0, qi + 1, value_pass, 0)
    o_ref[...] = (acc_scr[...] / jnp.sum(l_scr[...], axis=-1, keepdims=True)).astype(o_ref.dtype)


def _attention(q, k, v):
    B, S, _ = q.shape
    nq = S // TQ
    return pl.pallas_call(
        _attn_kernel,
        grid=(B, N_HEADS, nq),
        in_specs=[pl.BlockSpec((None, TQ, HEAD_PAD), lambda b, h, i: (b, i, h)),
                  pl.BlockSpec((None, S, HEAD_PAD), lambda b, h, i: (b, 0, h)),
                  pl.BlockSpec((None, S, V_DIM), lambda b, h, i: (b, 0, h))],
        out_specs=pl.BlockSpec((None, TQ, V_DIM), lambda b, h, i: (b, i, h)),
        out_shape=jax.ShapeDtypeStruct((B, S, ATTN_W), _bf16),
        scratch_shapes=[pltpu.VMEM((nq, TQ, TQ), _f32), pltpu.VMEM((TQ, LANES), _f32),
                        pltpu.VMEM((TQ, LANES), _f32), pltpu.VMEM((TQ, V_DIM), _f32)],
        compiler_params=pltpu.CompilerParams(dimension_semantics=("arbitrary", "arbitrary", "arbitrary"),
                                             vmem_limit_bytes=VMEM_LIMIT),
        name="attention",
    )(q, k, v)


def _post_kernel(x_ref, mix_ref, att_ref, aog_ref, wo_ref, fng_ref, wr_ref, br_ref,
                 x1_ref, h2_ref, route_ref, rgate_ref, cnt_ref):
    tm = x_ref.shape[0]
    att = att_ref[...].astype(_f32)
    att_n = (att * _rms(att, ATTN_W) * aog_ref[...]).astype(_bf16)
    x1 = x_ref[...] + _dot(mix_ref[...], wo_ref[0:POOL_W + CONV_W, :]) + _dot(att_n, wo_ref[POOL_W + CONV_W:, :])
    x1_ref[...] = x1
    h2 = (x1 * _rms(x1, D_MODEL) * fng_ref[...]).astype(_bf16)
    h2_ref[...] = h2

    logits = _dot(h2, wr_ref[...]) + br_ref[...]
    lane = lax.broadcasted_iota(jnp.int32, logits.shape, 1)
    lane_f = lane.astype(_f32)
    work = logits
    sel = jnp.zeros(logits.shape, _f32)
    onehots, vals, idxs = [], [], []
    for _ in range(TOP_K):
        mx = jnp.max(work, axis=-1, keepdims=True)
        idx_f = jnp.min(jnp.where(work == mx, lane_f, float(LANES)), axis=-1, keepdims=True)
        idx = idx_f.astype(jnp.int32)
        hot = lane == idx
        onehots.append(hot)
        vals.append(mx)
        idxs.append(idx)
        sel = jnp.where(hot, 1.0, sel)
        work = jnp.where(hot, -jnp.inf, work)
    exps = [jnp.exp(vk - vals[0]) for vk in vals]
    denom = exps[0] + exps[1] + exps[2] + exps[3]

    nt = tm // ROUTE_TILE
    r_i = lax.broadcasted_iota(jnp.int32, (ROUTE_TILE, ROUTE_TILE), 0)
    c_i = lax.broadcasted_iota(jnp.int32, (ROUTE_TILE, ROUTE_TILE), 1)
    tri = jnp.where(c_i < r_i, 1.0, 0.0).astype(_bf16)
    upper = jnp.where(r_i < c_i, 1.0, 0.0).astype(_bf16)
    sel_b = sel.astype(_bf16)
    ranks, cnts = [], []
    for s in range(nt):
        rows = slice(s * ROUTE_TILE, (s + 1) * ROUTE_TILE)
        ranks.append(_dot(tri, sel_b[rows, :]))
        cnts.append(jnp.sum(sel[rows, :], axis=0, keepdims=True))
    cnt = jnp.concatenate(cnts, axis=0)
    cnt_ref[...] = cnt
    chunks = jnp.floor((cnt + (ROW_CHUNK - 1)) * (1.0 / ROW_CHUNK))
    boff = _dot(chunks.astype(_bf16), upper) * float(ROW_CHUNK)
    slot = jnp.concatenate(
        [ranks[s] + jnp.broadcast_to(boff[s:s + 1, :], (ROUTE_TILE, LANES)) for s in range(nt)], axis=0)

    route = jnp.zeros(logits.shape, jnp.int32)
    rgate = jnp.zeros(logits.shape, _f32)
    for kk in range(TOP_K):
        col = jnp.sum(jnp.where(onehots[kk], slot, 0.0), axis=-1, keepdims=True).astype(jnp.int32)
        route = jnp.where(lane == kk, idxs[kk], route)
        route = jnp.where(lane == TOP_K + kk, col, route)
        rgate = jnp.where(lane == kk, exps[kk] / denom, rgate)
    route_ref[...] = route
    rgate_ref[...] = rgate


def _post_mix(layer, x, mix, att, p):
    T = x.shape[0]
    tm = TM_POST
    row = lambda w: pl.BlockSpec((tm, w), lambda i: (i, 0))
    names = ("attn_out_g", "w_o", "ffn_norm_g", "w_router", "b_router")
    weights = [p[n] for n in names]
    out_shape = (jax.ShapeDtypeStruct((T, D_MODEL), _f32), jax.ShapeDtypeStruct((T, D_MODEL), _bf16),
                 jax.ShapeDtypeStruct((T, LANES), jnp.int32), jax.ShapeDtypeStruct((T, LANES), _f32),
                 jax.ShapeDtypeStruct((T // ROUTE_TILE, LANES), _f32))
    return pl.pallas_call(
        _post_kernel,
        grid=(T // tm,),
        in_specs=[row(D_MODEL), row(POOL_W + CONV_W), row(ATTN_W)] + [_layer_spec(w.shape, layer, 1) for w in weights],
        out_specs=(row(D_MODEL), row(D_MODEL), row(LANES), row(LANES),
                   pl.BlockSpec((tm // ROUTE_TILE, LANES), lambda i: (i, 0))),
        out_shape=out_shape,
        compiler_params=pltpu.CompilerParams(dimension_semantics=("arbitrary",), vmem_limit_bytes=VMEM_LIMIT),
        name="post_mix",
    )(x, mix, att, *weights)


N_SLAB = D_MODEL // LANES
assert N_SLAB == SUBLANES
CHUNK_SUBROWS = ROW_CHUNK * N_SLAB


def _to_token_major(ref, value):
    rows = value.shape[0]
    for j in range(N_SLAB):
        ref[pl.ds(j, rows, stride=N_SLAB), :] = value[:, j * LANES:(j + 1) * LANES]


def _from_token_major(ref, rows):
    return jnp.concatenate([ref[pl.ds(j, rows, stride=N_SLAB), :] for j in range(N_SLAB)], axis=-1)


def _subrow(row):
    return pl.multiple_of(row * N_SLAB, N_SLAB)


MAX_TILE_CHUNKS = ROUTE_TILE * TOP_K // ROW_CHUNK + N_EXPERTS
assert MAX_TILE_CHUNKS * ROW_CHUNK == STAGE_ROWS
DUMP_ROWS = STAGE_ROWS


def _all_chunks(chunk_rows, tile, copy_fn):
    for c in range(MAX_TILE_CHUNKS):
        copy_fn(c * ROW_CHUNK, chunk_rows[tile * MAX_TILE_CHUNKS + c])


def _dispatch_kernel(chunk_rows, pad_start, pad_chunks, slot_ref, h2_ref, xs_hbm, stage, zeros, sems, zsem):
    i = pl.program_id(0)
    n_steps = pl.num_programs(0)
    cur = i % 2

    def chunk_copy(slot, stage_row, sorted_row):
        return pltpu.make_async_copy(stage.at[slot, pl.ds(_subrow(stage_row), CHUNK_SUBROWS)],
                                     xs_hbm.at[pl.ds(_subrow(sorted_row), CHUNK_SUBROWS)], sems.at[slot])

    def wait_tile(slot):
        for _ in range(MAX_TILE_CHUNKS):
            chunk_copy(slot, 0, 0).wait()

    @pl.when(i == 0)
    def _():
        zeros[...] = jnp.zeros(zeros.shape, _f32)

        def zero_copy(row):
            return pltpu.make_async_copy(zeros, xs_hbm.at[pl.ds(_subrow(row), CHUNK_SUBROWS)], zsem)

        for e in range(N_EXPERTS + 1):
            def start(c, carry, e=e):
                zero_copy(pad_start[e] + c * ROW_CHUNK).start()
                return carry
            lax.fori_loop(0, pad_chunks[e], start, 0)
        for e in range(N_EXPERTS + 1):
            def wait(c, carry, e=e):
                zero_copy(pad_start[e] + c * ROW_CHUNK).wait()
                return carry
            lax.fori_loop(0, pad_chunks[e], wait, 0)

    slots = slot_ref[...]
    row_i = lax.broadcasted_iota(jnp.int32, (STAGE_ROWS, ROUTE_TILE), 0)
    place = jnp.zeros((STAGE_ROWS, ROUTE_TILE), _f32)
    for kk in range(TOP_K):
        place = place + jnp.where(row_i == slots[kk:kk + 1, :], 1.0, 0.0)
    staged = _dot(place.astype(_bf16), h2_ref[...])
    _to_token_major(stage.at[cur], staged)

    @pl.when(i > 0)
    def _():
        wait_tile(1 - cur)

    _all_chunks(chunk_rows, i, lambda sr, dr: chunk_copy(cur, sr, dr).start())

    @pl.when(i == n_steps - 1)
    def _():
        wait_tile(cur)


def _dispatch(h2, slots_t, sched, n_rows):
    T = h2.shape[0]
    grid_spec = pltpu.PrefetchScalarGridSpec(
        num_scalar_prefetch=3,
        grid=(T // ROUTE_TILE,),
        in_specs=[pl.BlockSpec((None, SUBLANES, ROUTE_TILE), lambda i, *_: (i, 0, 0)),
                  pl.BlockSpec((ROUTE_TILE, D_MODEL), lambda i, *_: (i, 0))],
        out_specs=pl.BlockSpec(memory_space=pl.ANY),
        scratch_shapes=[pltpu.VMEM((2, STAGE_ROWS * N_SLAB, LANES), _f32),
                        pltpu.VMEM((CHUNK_SUBROWS, LANES), _f32),
                        pltpu.SemaphoreType.DMA((2,)), pltpu.SemaphoreType.DMA(())],
    )
    return pl.pallas_call(
        _dispatch_kernel,
        grid_spec=grid_spec,
        out_shape=jax.ShapeDtypeStruct(((n_rows + DUMP_ROWS) * N_SLAB, LANES), _f32),
        compiler_params=pltpu.CompilerParams(dimension_semantics=("arbitrary",), vmem_limit_bytes=VMEM_LIMIT),
        name="dispatch",
    )(sched["write_rows"], sched["pad_start"], sched["pad_chunks"], slots_t, h2)


GU_GROUP = 2 * LANES


def _expert_kernel(blk_exp, blk_valid, xs_ref, wgu_ref, bgu_ref, wd_ref, bd_ref, y_ref, wgu_s, wd_s):
    i = pl.program_id(0)
    valid = blk_valid[i]

    @pl.when(jnp.logical_or(i == 0, blk_exp[i] != blk_exp[jnp.maximum(i - 1, 0)]))
    def _():
        r_i = lax.broadcasted_iota(jnp.int32, (GU_GROUP, GU_GROUP), 0)
        c_i = lax.broadcasted_iota(jnp.int32, (GU_GROUP, GU_GROUP), 1)
        src = jnp.where(c_i < LANES, 2 * c_i, 2 * (c_i - LANES) + 1)
        perm = jnp.where(r_i == src, 1.0, 0.0).astype(_bf16)
        for b in range(2 * D_FF // GU_GROUP):
            cols = slice(b * GU_GROUP, (b + 1) * GU_GROUP)
            wgu_s[:, cols] = _dot(wgu_ref[:, cols].astype(_bf16), perm).astype(_bf16)
        wd_s[...] = wd_ref[...].astype(_bf16)

    @pl.when(valid > 0)
    def _():
        x = _from_token_major(xs_ref, EXPERT_BLOCK).astype(_bf16)
        y = jnp.zeros((EXPERT_BLOCK, D_MODEL), _f32)
        groups = FF_CHUNK // LANES
        for c in range(D_FF // FF_CHUNK):
            lo = 2 * c * FF_CHUNK
            h = _dot(x, wgu_s[:, lo:lo + 2 * FF_CHUNK]) + bgu_ref[:, lo:lo + 2 * FF_CHUNK]
            hg = jnp.concatenate([h[:, (2 * q) * LANES:(2 * q + 1) * LANES] for q in range(groups)], axis=-1)
            hl = jnp.concatenate([h[:, (2 * q + 1) * LANES:(2 * q + 2) * LANES] for q in range(groups)], axis=-1)
            g = jnp.minimum(hg, SWIGLU_LIMIT)
            lin = jnp.clip(hl, -SWIGLU_LIMIT, SWIGLU_LIMIT)
            act = g * jax.nn.sigmoid(SWIGLU_ALPHA * g) * (lin + 1.0)
            y = y + _dot(act.astype(_bf16), wd_s[c * FF_CHUNK:(c + 1) * FF_CHUNK, :])
        _to_token_major(y_ref, y + bd_ref[...])

    @pl.when(valid == 0)
    def _():
        y_ref[...] = jnp.zeros(y_ref.shape, _f32)


def _experts(layer, xs, n_rows, blk_exp, blk_valid, w_gu, b_gu, w_down, b_down):
    nb = n_rows // EXPERT_BLOCK
    grid_spec = pltpu.PrefetchScalarGridSpec(
        num_scalar_prefetch=2,
        grid=(nb,),
        in_specs=[pl.BlockSpec((EXPERT_BLOCK * N_SLAB, LANES), lambda i, be, bv: (i, 0)),
                  pl.BlockSpec((None, None, D_MODEL, 2 * D_FF), lambda i, be, bv: (layer, be[i], 0, 0)),
                  pl.BlockSpec((None, None, 1, 2 * D_FF), lambda i, be, bv: (layer, be[i], 0, 0)),
                  pl.BlockSpec((None, None, D_FF, D_MODEL), lambda i, be, bv: (layer, be[i], 0, 0)),
                  pl.BlockSpec((None, None, 1, D_MODEL), lambda i, be, bv: (layer, be[i], 0, 0))],
        out_specs=pl.BlockSpec((EXPERT_BLOCK * N_SLAB, LANES), lambda i, be, bv: (i, 0)),
        scratch_shapes=[pltpu.VMEM((D_MODEL, 2 * D_FF), _bf16), pltpu.VMEM((D_FF, D_MODEL), _bf16)],
    )
    return pl.pallas_call(
        _expert_kernel,
        grid_spec=grid_spec,
        out_shape=jax.ShapeDtypeStruct((n_rows * N_SLAB, LANES), _f32),
        compiler_params=pltpu.CompilerParams(dimension_semantics=("arbitrary",), vmem_limit_bytes=VMEM_LIMIT),
        name="experts",
    )(blk_exp, blk_valid, xs, w_gu, b_gu, w_down, b_down)


def _combine_kernel(chunk_rows, x1_ref, route_ref, gate_ref, ys_hbm, out_ref, stage, sems):
    i = pl.program_id(0)
    n_steps = pl.num_programs(0)
    cur = i % 2

    def chunk_copy(slot, stage_row, sorted_row):
        return pltpu.make_async_copy(ys_hbm.at[pl.ds(_subrow(sorted_row), CHUNK_SUBROWS)],
                                     stage.at[slot, pl.ds(_subrow(stage_row), CHUNK_SUBROWS)], sems.at[slot])

    def fetch_tile(tile, slot):
        _all_chunks(chunk_rows, tile, lambda sr, dr: chunk_copy(slot, sr, dr).start())

    @pl.when(i == 0)
    def _():
        fetch_tile(0, 0)

    @pl.when(i + 1 < n_steps)
    def _():
        fetch_tile(i + 1, 1 - cur)

    for _ in range(MAX_TILE_CHUNKS):
        chunk_copy(cur, 0, 0).wait()

    rows = _from_token_major(stage.at[cur], STAGE_ROWS).astype(_bf16)
    route = route_ref[...]
    gates = gate_ref[...]
    lane_i = lax.broadcasted_iota(jnp.int32, (ROUTE_TILE, STAGE_ROWS), 1)
    weight = jnp.zeros((ROUTE_TILE, STAGE_ROWS), _f32)
    for kk in range(TOP_K):
        weight = weight + jnp.where(lane_i == route[:, TOP_K + kk:TOP_K + kk + 1], gates[:, kk:kk + 1], 0.0)
    out_ref[...] = x1_ref[...] + _dot(weight.astype(_bf16), rows)


def _combine(x1, route, rgate, ys, sched):
    T = x1.shape[0]
    tc = ROUTE_TILE
    grid_spec = pltpu.PrefetchScalarGridSpec(
        num_scalar_prefetch=1,
        grid=(T // tc,),
        in_specs=[pl.BlockSpec((tc, D_MODEL), lambda i, *_: (i, 0)),
                  pl.BlockSpec((tc, LANES), lambda i, *_: (i, 0)),
                  pl.BlockSpec((tc, LANES), lambda i, *_: (i, 0)),
                  pl.BlockSpec(memory_space=pl.ANY)],
        out_specs=pl.BlockSpec((tc, D_MODEL), lambda i, *_: (i, 0)),
        scratch_shapes=[pltpu.VMEM((2, STAGE_ROWS * N_SLAB, LANES), _f32),
                        pltpu.SemaphoreType.DMA((2,))],
    )
    return pl.pallas_call(
        _combine_kernel,
        grid_spec=grid_spec,
        out_shape=jax.ShapeDtypeStruct((T, D_MODEL), _f32),
        compiler_params=pltpu.CompilerParams(dimension_semantics=("arbitrary",), vmem_limit_bytes=VMEM_LIMIT),
        name="combine",
    )(sched["read_rows"], x1, route, rgate, ys)


def _prepare_params(w):
    L = w["w_in"].shape[0]
    p = {}
    row = lambda a: a.reshape(L, 1, a.shape[-1])
    p["attn_norm_g"] = row(w["attn_norm_g"])
    p["w_in"] = jnp.pad(w["w_in"], ((0, 0), (0, 0), (0, D_IN_PAD - w["w_in"].shape[-1]))).astype(_bf16)
    eye = jnp.eye(len(POOL_WINDOWS), dtype=_f32)
    p["pool_w"] = jnp.einsum("lgcd,gh->lgchd", w["pool_w"], eye).reshape(L, POOL_W, POOL_W).astype(_bf16)
    p["pool_scale"] = row(w["pool_scale"])
    p["conv_dw"] = w["conv_dw"]
    p["conv_dw_b"] = row(w["conv_dw_b"])
    p["conv_ln_g"] = row(w["conv_ln_g"])
    p["conv_ln_b"] = row(w["conv_ln_b"])
    p["conv_pw"] = w["conv_pw"].astype(_bf16)
    p["conv_pw_b"] = row(w["conv_pw_b"])
    p["q_a_g"] = row(w["q_a_g"])
    wuq = w["w_uq"].reshape(L, Q_LORA, N_HEADS, QK_HEAD)
    p["w_uq"] = jnp.pad(wuq, ((0, 0), (0, 0), (0, 0), (0, HEAD_PAD - QK_HEAD))).reshape(
        L, Q_LORA, N_HEADS * HEAD_PAD).astype(_bf16)
    p["kv_a_g"] = row(w["kv_a_g"])
    wukv = w["w_ukv"].reshape(L, KV_LORA, N_HEADS, QK_NOPE + V_DIM)
    p["w_ukv"] = jnp.concatenate([wukv[..., :QK_NOPE].reshape(L, KV_LORA, N_HEADS * QK_NOPE),
                                  wukv[..., QK_NOPE:].reshape(L, KV_LORA, N_HEADS * V_DIM)], axis=-1).astype(_bf16)
    pad_head = lambda g: jnp.pad(g, ((0, 0), (0, HEAD_PAD - QK_HEAD))).reshape(L, 1, HEAD_PAD)
    p["q_norm_g"] = pad_head(w["q_norm_g"])
    p["k_norm_g"] = pad_head(w["k_norm_g"])
    p["pool_out_g"] = row(w["pool_out_g"])
    p["conv_out_g"] = row(w["conv_out_g"])
    p["attn_out_g"] = row(w["attn_out_g"])
    p["w_o"] = w["w_o"].astype(_bf16)
    p["ffn_norm_g"] = row(w["ffn_norm_g"])
    p["w_router"] = jnp.pad(w["w_router"], ((0, 0), (0, 0), (0, LANES - N_EXPERTS))).astype(_bf16)
    p["b_router"] = jnp.pad(w["b_router"], ((0, 0), (0, LANES - N_EXPERTS)),
                            constant_values=NEG_BIG).reshape(L, 1, LANES)
    E = w["w_gu"].shape[1]
    p["w_gu"] = w["w_gu"]
    bgu = w["b_gu"].reshape(L, E, 2 * D_FF // GU_GROUP, LANES, 2)
    p["b_gu"] = jnp.swapaxes(bgu, -1, -2).reshape(L, E, 1, 2 * D_FF)
    p["w_down"] = w["w_down"]
    p["b_down"] = w["b_down"].reshape(L, E, 1, D_MODEL)
    return p


def _rope_tables(positions):
    inv_freq = 1.0 / (ROPE_THETA ** (jnp.arange(0, QK_ROPE, 2, dtype=_f32) / QK_ROPE))
    ang = positions.astype(_f32)[..., None] * inv_freq
    cos, sin = jnp.cos(ang), jnp.sin(ang)
    zeros = jnp.zeros(cos.shape[:-1] + (LANES - QK_ROPE,), _f32)
    return (jnp.concatenate([cos, cos, zeros], axis=-1), jnp.concatenate([-sin, sin, zeros], axis=-1))


def _routing_schedule(cnt_f, n_tokens):
    n = cnt_f[:, 0:N_EXPERTS].astype(jnp.int32)
    counts = jnp.sum(n, axis=0)
    spare = ROW_CHUNK - 1
    padded = jnp.where(counts > 0, (counts + spare + EXPERT_BLOCK - 1) // EXPERT_BLOCK * EXPERT_BLOCK, 0)
    pend = jnp.cumsum(padded)
    pstart = pend - padded
    base = pstart[None, :] + jnp.cumsum(n, axis=0) - n
    n_chunks = (n + ROW_CHUNK - 1) // ROW_CHUNK
    max_rows = n_tokens * TOP_K + N_EXPERTS * (spare + EXPERT_BLOCK - 1) + ROW_CHUNK
    n_blocks = -(-max_rows // EXPERT_BLOCK)
    blk_row = jnp.arange(n_blocks, dtype=jnp.int32) * EXPERT_BLOCK
    blk_exp = jnp.minimum(jnp.sum((blk_row[:, None] >= pend[None, :]).astype(jnp.int32), axis=1), N_EXPERTS - 1)
    onehot = (blk_exp[:, None] == jnp.arange(N_EXPERTS, dtype=jnp.int32)[None, :]).astype(jnp.int32)
    seg_end = jnp.sum(onehot * (pstart + counts)[None, :], axis=1)
    blk_valid = jnp.where(blk_row < pend[-1], jnp.clip(seg_end - blk_row, 0, EXPERT_BLOCK), 0)
    i32 = lambda a: a.astype(jnp.int32)
    n_rows = n_blocks * EXPERT_BLOCK
    seg_chunks = (padded - counts + ROW_CHUNK - 1) // ROW_CHUNK
    pad_start = jnp.concatenate([pend - seg_chunks * ROW_CHUNK, pend[-1:]])
    pad_chunks = jnp.concatenate([seg_chunks, (n_rows + DUMP_ROWS - pend[-1:]) // ROW_CHUNK])
    cum_end = jnp.cumsum(n_chunks, axis=1)
    cum_start = cum_end - n_chunks
    c_idx = jnp.arange(MAX_TILE_CHUNKS, dtype=jnp.int32)
    e_of = jnp.minimum(jnp.sum((c_idx[None, :, None] >= cum_end[:, None, :]).astype(jnp.int32), axis=-1),
                       N_EXPERTS - 1)
    pick = (e_of[:, :, None] == jnp.arange(N_EXPERTS, dtype=jnp.int32)[None, None, :]).astype(jnp.int32)
    chunk_dst = (jnp.sum(pick * base[:, None, :], axis=-1)
                 + (c_idx[None, :] - jnp.sum(pick * cum_start[:, None, :], axis=-1)) * ROW_CHUNK)
    needed = c_idx[None, :] < cum_end[:, -1:]
    write_rows = jnp.where(needed, chunk_dst, n_rows + c_idx[None, :] * ROW_CHUNK)
    read_rows = jnp.where(needed, chunk_dst, 0)
    sched = dict(write_rows=i32(write_rows).reshape(-1), read_rows=i32(read_rows).reshape(-1),
                 pad_start=i32(pad_start), pad_chunks=i32(pad_chunks))
    return sched, i32(blk_exp), i32(blk_valid), n_rows


def kernel(x, positions, attn_norm_g, w_in, pool_w, pool_scale, conv_dw, conv_dw_b, conv_ln_g, conv_ln_b, conv_pw, conv_pw_b, q_a_g, w_uq, kv_a_g, w_ukv, q_norm_g, k_norm_g, pool_out_g, conv_out_g, attn_out_g, w_o, ffn_norm_g, w_router, b_router, w_gu, b_gu, w_down, b_down):
    B, S, D = x.shape
    T = B * S
    depth = w_in.shape[0]
    p = _prepare_params(dict(
        attn_norm_g=attn_norm_g, w_in=w_in, pool_w=pool_w, pool_scale=pool_scale, conv_dw=conv_dw,
        conv_dw_b=conv_dw_b, conv_ln_g=conv_ln_g, conv_ln_b=conv_ln_b, conv_pw=conv_pw, conv_pw_b=conv_pw_b,
        q_a_g=q_a_g, w_uq=w_uq, kv_a_g=kv_a_g, w_ukv=w_ukv, q_norm_g=q_norm_g, k_norm_g=k_norm_g,
        pool_out_g=pool_out_g, conv_out_g=conv_out_g, attn_out_g=attn_out_g, w_o=w_o, ffn_norm_g=ffn_norm_g,
        w_router=w_router, b_router=b_router, w_gu=w_gu, b_gu=b_gu, w_down=w_down, b_down=b_down))
    rope_c, rope_s = _rope_tables(positions)
    for layer in range(depth):
        mix, q, k, v = _mixer_front(layer, x, rope_c, rope_s, p)
        att = _attention(q, k, v)
        x1, h2, route, rgate, cnt = _post_mix(layer, x.reshape(T, D), mix.reshape(T, -1), att.reshape(T, -1), p)
        sched, blk_exp, blk_valid, n_rows = _routing_schedule(cnt, T)
        slots_t = route[:, TOP_K:2 * TOP_K].reshape(T // ROUTE_TILE, ROUTE_TILE, TOP_K).transpose(0, 2, 1)
        slots_t = jnp.pad(slots_t, ((0, 0), (0, SUBLANES - TOP_K), (0, 0)), constant_values=-1)
        xs = _dispatch(h2, slots_t, sched, n_rows)
        ys = _experts(layer, xs, n_rows, blk_exp, blk_valid, p["w_gu"], p["b_gu"], p["w_down"], p["b_down"])
        x = _combine(x1, route, rgate, ys, sched).reshape(B, S, D)
    return x
```

```python
import functools
import math

import jax
import jax.numpy as jnp
from jax import lax
from jax.experimental import pallas as pl
from jax.experimental.pallas import tpu as pltpu

D_MODEL = 1024
CHUNK = 64
POOL_W = 256
POOL_WINDOWS = (2, 4, 8, 16)
POOL_GC = 64
CONV_W = 256
CONV_K = 31
N_HEADS = 4
QK_NOPE = 128
QK_ROPE = 64
QK_HEAD = QK_NOPE + QK_ROPE
V_DIM = 128
Q_LORA = 256
KV_LORA = 128
ATTN_W = N_HEADS * V_DIM
ROPE_THETA = 10000.0
N_EXPERTS = 32
TOP_K = 4
D_FF = D_MODEL
SWIGLU_LIMIT = 7.0
SWIGLU_ALPHA = 1.702
EPS = 1e-6

LANES = 128
HEAD_PAD = 2 * LANES
D_IN_PAD = 1280
POOL_HALO = 16
CONV_HALO = 32
NEG_BIG = -1e30

TM_FRONT = 512
TQ = 512
TM_POST = 1024
ROUTE_TILE = 128
ROW_CHUNK = 8
SUBLANES = 8
STAGE_ROWS = -(-(ROUTE_TILE * TOP_K + N_EXPERTS * (ROW_CHUNK - 1)) // LANES) * LANES
EXPERT_BLOCK = 512
FF_CHUNK = 256
VMEM_LIMIT = 56 * 1024 * 1024

_bf16 = jnp.bfloat16
_f32 = jnp.float32


def _dot(a, b):
    return jnp.dot(a, b, preferred_element_type=_f32)


def _rms(x, width):
    return lax.rsqrt(jnp.sum(x * x, axis=-1, keepdims=True) * (1.0 / width) + EPS)


def _layer_spec(shape, layer, n_grid):
    nd = len(shape)
    block = (None,) + tuple(shape[1:])
    if n_grid == 1:
        return pl.BlockSpec(block, lambda i: (layer,) + (0,) * (nd - 1))
    return pl.BlockSpec(block, lambda i, j: (layer,) + (0,) * (nd - 1))


def _rope(x, c_tab, s_tab):
    lane = lax.broadcasted_iota(jnp.int32, x.shape, 1)
    partner = jnp.where(lane < QK_ROPE // 2, pltpu.roll(x, LANES - QK_ROPE // 2, 1),
                        pltpu.roll(x, QK_ROPE // 2, 1))
    return x * c_tab + partner * s_tab


def _front_kernel(x_ref, ropec_ref, ropes_ref, ang_ref, win_ref, poolw_ref, pscale_ref, cdw_ref, cdwb_ref,
                  clng_ref, clnb_ref, cpw_ref, cpwb_ref, qag_ref, wuq_ref, kvag_ref, wukv_ref, qng_ref,
                  kng_ref, poutg_ref, coutg_ref,
                  mix_ref, q_ref, k_ref, v_ref, pool_ext, conv_ext):
    j = pl.program_id(1)
    tm = x_ref.shape[0]

    x = x_ref[...]
    h = x * _rms(x, D_MODEL) * ang_ref[...]
    z = _dot(h.astype(_bf16), win_ref[...])

    u = z[:, 0:POOL_W]

    @pl.when(j == 0)
    def _():
        pool_ext[0:POOL_HALO, :] = jnp.zeros((POOL_HALO, POOL_W), _f32)
        conv_ext[0:CONV_HALO, :] = jnp.zeros((CONV_HALO, CONV_W), _f32)

    @pl.when(j > 0)
    def _():
        pool_ext[0:POOL_HALO, :] = pool_ext[tm:tm + POOL_HALO, :]
        conv_ext[0:CONV_HALO, :] = conv_ext[tm:tm + CONV_HALO, :]

    pool_ext[POOL_HALO:POOL_HALO + tm, :] = u
    t_pos = (j * tm + lax.broadcasted_iota(jnp.int32, (tm, LANES), 0) + 1).astype(_f32)
    lane = lax.broadcasted_iota(jnp.int32, (tm, LANES), 1)
    pooled_halves = []
    for half, (w_lo, w_hi) in enumerate(((POOL_WINDOWS[0], POOL_WINDOWS[1]),
                                         (POOL_WINDOWS[2], POOL_WINDOWS[3]))):
        cols = slice(half * LANES, (half + 1) * LANES)
        s_lo = pool_ext[POOL_HALO:POOL_HALO + tm, cols]
        for d in range(1, w_lo):
            s_lo = s_lo + pool_ext[POOL_HALO - d:POOL_HALO - d + tm, cols]
        s_hi = s_lo
        for d in range(w_lo, w_hi):
            s_hi = s_hi + pool_ext[POOL_HALO - d:POOL_HALO - d + tm, cols]
        first = lane < POOL_GC
        win_sum = jnp.where(first, s_lo, s_hi)
        cnt = jnp.minimum(t_pos, jnp.where(first, float(w_lo), float(w_hi)))
        pooled_halves.append(win_sum / cnt - u[:, cols])
    pooled = jnp.concatenate(pooled_halves, axis=-1)
    y_pool = _dot(pooled.astype(_bf16), poolw_ref[...]) * pscale_ref[...]
    y_pool = y_pool * _rms(y_pool, POOL_W) * poutg_ref[...]

    a = z[:, POOL_W:POOL_W + CONV_W]
    gate = z[:, POOL_W + CONV_W:POOL_W + 2 * CONV_W]
    conv_ext[CONV_HALO:CONV_HALO + tm, :] = a * jax.nn.sigmoid(gate)
    base = CONV_HALO - (CONV_K - 1)
    acc = cdwb_ref[...]
    window = conv_ext[...]
    n_win = window.shape[0]
    for r in range(SUBLANES):
        tiles = [(base + kk) // SUBLANES for kk in range(CONV_K) if (base + kk) % SUBLANES == r]
        shifted = window if r == 0 else pltpu.roll(window, n_win - r, 0)
        for m in tiles:
            kk = m * SUBLANES + r - base
            acc = acc + shifted[m * SUBLANES:m * SUBLANES + tm, :] * cdw_ref[kk:kk + 1, :]
    mu = jnp.mean(acc, axis=-1, keepdims=True)
    cen = acc - mu
    var = jnp.mean(cen * cen, axis=-1, keepdims=True)
    ln = cen * lax.rsqrt(var + EPS) * clng_ref[...] + clnb_ref[...]
    sw = ln * jax.nn.sigmoid(ln)
    y_conv = _dot(sw.astype(_bf16), cpw_ref[...]) + cpwb_ref[...]
    y_conv = y_conv * _rms(y_conv, CONV_W) * coutg_ref[...]

    mix_ref[:, 0:POOL_W] = y_pool.astype(_bf16)
    mix_ref[:, POOL_W:POOL_W + CONV_W] = y_conv.astype(_bf16)

    s1 = POOL_W + 2 * CONV_W
    c_q = z[:, s1:s1 + Q_LORA]
    c_kv = z[:, s1 + Q_LORA:s1 + Q_LORA + KV_LORA]
    k_rope = z[:, s1 + Q_LORA + KV_LORA:s1 + Q_LORA + KV_LORA + LANES]
    c_tab = ropec_ref[...]
    s_tab = ropes_ref[...]

    q_all = _dot((c_q * _rms(c_q, Q_LORA) * qag_ref[...]).astype(_bf16), wuq_ref[...])
    kv_all = _dot((c_kv * _rms(c_kv, KV_LORA) * kvag_ref[...]).astype(_bf16), wukv_ref[...])
    sm_scale = math.log2(math.e) / math.sqrt(QK_HEAD)
    kr_ss = jnp.sum(k_rope * k_rope, axis=-1, keepdims=True)
    kr_rot = _rope(k_rope * kng_ref[:, LANES:2 * LANES], c_tab, s_tab)
    for hd in range(N_HEADS):
        qh = q_all[:, hd * HEAD_PAD:(hd + 1) * HEAD_PAD]
        qh = qh * (_rms(qh, QK_HEAD) * sm_scale) * qng_ref[...]
        q_ref[:, hd * HEAD_PAD:hd * HEAD_PAD + LANES] = qh[:, 0:LANES].astype(_bf16)
        q_ref[:, hd * HEAD_PAD + LANES:(hd + 1) * HEAD_PAD] = _rope(qh[:, LANES:], c_tab, s_tab).astype(_bf16)
        kn = kv_all[:, hd * QK_NOPE:(hd + 1) * QK_NOPE]
        k_rs = lax.rsqrt((jnp.sum(kn * kn, axis=-1, keepdims=True) + kr_ss) * (1.0 / QK_HEAD) + EPS)
        k_ref[:, hd * HEAD_PAD:hd * HEAD_PAD + LANES] = (kn * k_rs * kng_ref[:, 0:LANES]).astype(_bf16)
        k_ref[:, hd * HEAD_PAD + LANES:(hd + 1) * HEAD_PAD] = (kr_rot * k_rs).astype(_bf16)
        v_lo = hd * (V_DIM + LANES)
        v_ref[:, v_lo:v_lo + V_DIM] = kv_all[:, (N_HEADS + hd) * V_DIM:(N_HEADS + hd + 1) * V_DIM].astype(_bf16)
        v_ref[:, v_lo + V_DIM:v_lo + V_DIM + LANES] = jnp.ones((tm, LANES), _bf16)


def _mixer_front(layer, x, rope_c, rope_s, p):
    B, S, _ = x.shape
    tm = TM_FRONT
    row = lambda w: pl.BlockSpec((None, tm, w), lambda b, j: (b, j, 0))
    names = ("attn_norm_g", "w_in", "pool_w", "pool_scale", "conv_dw", "conv_dw_b", "conv_ln_g", "conv_ln_b",
             "conv_pw", "conv_pw_b", "q_a_g", "w_uq", "kv_a_g", "w_ukv", "q_norm_g", "k_norm_g",
             "pool_out_g", "conv_out_g")
    weights = [p[n] for n in names]
    in_specs = [row(D_MODEL), row(LANES), row(LANES)] + [_layer_spec(w.shape, layer, 2) for w in weights]
    out_shape = (jax.ShapeDtypeStruct((B, S, POOL_W + CONV_W), _bf16),
                 jax.ShapeDtypeStruct((B, S, N_HEADS * HEAD_PAD), _bf16),
                 jax.ShapeDtypeStruct((B, S, N_HEADS * HEAD_PAD), _bf16),
                 jax.ShapeDtypeStruct((B, S, N_HEADS * (V_DIM + LANES)), _bf16))
    out_specs = (row(POOL_W + CONV_W), row(N_HEADS * HEAD_PAD), row(N_HEADS * HEAD_PAD),
                 row(N_HEADS * (V_DIM + LANES)))
    return pl.pallas_call(
        _front_kernel,
        grid=(B, S // tm),
        in_specs=in_specs,
        out_specs=out_specs,
        out_shape=out_shape,
        scratch_shapes=[pltpu.VMEM((tm + POOL_HALO, POOL_W), _f32),
                        pltpu.VMEM((tm + CONV_HALO, CONV_W), _f32)],
        compiler_params=pltpu.CompilerParams(dimension_semantics=("arbitrary", "arbitrary"),
                                             vmem_limit_bytes=VMEM_LIMIT),
        name="mixer_front",
    )(x, rope_c, rope_s, *weights)


def _attn_kernel(q_ref, k_ref, v_ref, o_ref):
    qi = pl.program_id(2)
    tq = q_ref.shape[0]
    q_chunk = lax.broadcasted_iota(jnp.int32, (tq, tq), 0) // CHUNK
    k_chunk = lax.broadcasted_iota(jnp.int32, (tq, tq), 1) // CHUNK

    def tile_body(n):
        keys = (n + 1) * tq
        s = lax.dot_general(q_ref[...], k_ref[0:keys, :], (((1,), (1,)), ((), ())),
                            preferred_element_type=_f32)
        diag = jnp.where(k_chunk <= q_chunk, s[:, n * tq:], NEG_BIG)
        s = diag if n == 0 else jnp.concatenate([s[:, :n * tq], diag], axis=-1)
        p = jnp.exp2(s - jnp.max(s, axis=-1, keepdims=True)).astype(_bf16)
        acc = _dot(p, v_ref[0:keys, :])
        o_ref[...] = (acc[:, 0:V_DIM] / acc[:, V_DIM:V_DIM + 1]).astype(o_ref.dtype)

    for n in range(k_ref.shape[0] // tq):
        pl.when(qi == n)(functools.partial(tile_body, n))


def _attention(q, k, v):
    B, S, _ = q.shape
    nq = S // TQ
    return pl.pallas_call(
        _attn_kernel,
        grid=(B, N_HEADS, nq),
        in_specs=[pl.BlockSpec((None, TQ, HEAD_PAD), lambda b, h, i: (b, i, h)),
                  pl.BlockSpec((None, S, HEAD_PAD), lambda b, h, i: (b, 0, h)),
                  pl.BlockSpec((None, S, V_DIM + LANES), lambda b, h, i: (b, 0, h))],
        out_specs=pl.BlockSpec((None, TQ, V_DIM), lambda b, h, i: (b, i, h)),
        out_shape=jax.ShapeDtypeStruct((B, S, ATTN_W), _bf16),
        compiler_params=pltpu.CompilerParams(dimension_semantics=("arbitrary", "arbitrary", "arbitrary"),
                                             vmem_limit_bytes=VMEM_LIMIT),
        name="attention",
    )(q, k, v)


def _post_kernel(x_ref, mix_ref, att_ref, aog_ref, wo_ref, fng_ref, wr_ref, br_ref,
                 x1_ref, h2_ref, route_ref, rgate_ref, cnt_ref):
    tm = x_ref.shape[0]
    att = att_ref[...].astype(_f32)
    att_n = (att * _rms(att, ATTN_W) * aog_ref[...]).astype(_bf16)
    x1 = x_ref[...] + _dot(mix_ref[...], wo_ref[0:POOL_W + CONV_W, :]) + _dot(att_n, wo_ref[POOL_W + CONV_W:, :])
    x1_ref[...] = x1
    h2 = (x1 * _rms(x1, D_MODEL) * fng_ref[...]).astype(_bf16)
    h2_ref[...] = h2

    logits = _dot(h2, wr_ref[...]) + br_ref[...]
    lane = lax.broadcasted_iota(jnp.int32, logits.shape, 1)
    lane_f = lane.astype(_f32)
    work = logits
    sel = jnp.zeros(logits.shape, _f32)
    onehots, vals, idxs = [], [], []
    for _ in range(TOP_K):
        mx = jnp.max(work, axis=-1, keepdims=True)
        idx_f = jnp.min(jnp.where(work == mx, lane_f, float(LANES)), axis=-1, keepdims=True)
        idx = idx_f.astype(jnp.int32)
        hot = lane == idx
        onehots.append(hot)
        vals.append(mx)
        idxs.append(idx)
        sel = jnp.where(hot, 1.0, sel)
        work = jnp.where(hot, -jnp.inf, work)
    exps = [jnp.exp(vk - vals[0]) for vk in vals]
    denom = exps[0] + exps[1] + exps[2] + exps[3]

    nt = tm // ROUTE_TILE
    r_i = lax.broadcasted_iota(jnp.int32, (ROUTE_TILE, ROUTE_TILE), 0)
    c_i = lax.broadcasted_iota(jnp.int32, (ROUTE_TILE, ROUTE_TILE), 1)
    tri = jnp.where(c_i < r_i, 1.0, 0.0).astype(_bf16)
    upper = jnp.where(r_i < c_i, 1.0, 0.0).astype(_bf16)
    sel_b = sel.astype(_bf16)
    ranks, cnts = [], []
    for s in range(nt):
        rows = slice(s * ROUTE_TILE, (s + 1) * ROUTE_TILE)
        ranks.append(_dot(tri, sel_b[rows, :]))
        cnts.append(jnp.sum(sel[rows, :], axis=0, keepdims=True))
    cnt = jnp.concatenate(cnts, axis=0)
    cnt_ref[...] = cnt
    chunks = jnp.floor((cnt + (ROW_CHUNK - 1)) * (1.0 / ROW_CHUNK))
    boff = _dot(chunks.astype(_bf16), upper) * float(ROW_CHUNK)
    slot = jnp.concatenate(
        [ranks[s] + jnp.broadcast_to(boff[s:s + 1, :], (ROUTE_TILE, LANES)) for s in range(nt)], axis=0)

    route = jnp.zeros(logits.shape, jnp.int32)
    rgate = jnp.zeros(logits.shape, _f32)
    for kk in range(TOP_K):
        col = jnp.sum(jnp.where(onehots[kk], slot, 0.0), axis=-1, keepdims=True).astype(jnp.int32)
        route = jnp.where(lane == kk, idxs[kk], route)
        route = jnp.where(lane == TOP_K + kk, col, route)
        rgate = jnp.where(lane == kk, exps[kk] / denom, rgate)
    route_ref[...] = route
    rgate_ref[...] = rgate


def _post_mix(layer, x, mix, att, p):
    T = x.shape[0]
    tm = TM_POST
    row = lambda w: pl.BlockSpec((tm, w), lambda i: (i, 0))
    names = ("attn_out_g", "w_o", "ffn_norm_g", "w_router", "b_router")
    weights = [p[n] for n in names]
    out_shape = (jax.ShapeDtypeStruct((T, D_MODEL), _f32), jax.ShapeDtypeStruct((T, D_MODEL), _bf16),
                 jax.ShapeDtypeStruct((T, LANES), jnp.int32), jax.ShapeDtypeStruct((T, LANES), _f32),
                 jax.ShapeDtypeStruct((T // ROUTE_TILE, LANES), _f32))
    return pl.pallas_call(
        _post_kernel,
        grid=(T // tm,),
        in_specs=[row(D_MODEL), row(POOL_W + CONV_W), row(ATTN_W)] + [_layer_spec(w.shape, layer, 1) for w in weights],
        out_specs=(row(D_MODEL), row(D_MODEL), row(LANES), row(LANES),
                   pl.BlockSpec((tm // ROUTE_TILE, LANES), lambda i: (i, 0))),
        out_shape=out_shape,
        compiler_params=pltpu.CompilerParams(dimension_semantics=("arbitrary",), vmem_limit_bytes=VMEM_LIMIT),
        name="post_mix",
    )(x, mix, att, *weights)


N_SLAB = D_MODEL // LANES
assert N_SLAB == SUBLANES
CHUNK_SUBROWS = ROW_CHUNK * N_SLAB


def _to_token_major(ref, value):
    rows = value.shape[0]
    for j in range(N_SLAB):
        ref[pl.ds(j, rows, stride=N_SLAB), :] = value[:, j * LANES:(j + 1) * LANES]


def _from_token_major(ref, rows):
    return jnp.concatenate([ref[pl.ds(j, rows, stride=N_SLAB), :] for j in range(N_SLAB)], axis=-1)


def _subrow(row):
    return pl.multiple_of(row * N_SLAB, N_SLAB)


MAX_TILE_CHUNKS = ROUTE_TILE * TOP_K // ROW_CHUNK + N_EXPERTS
assert MAX_TILE_CHUNKS * ROW_CHUNK == STAGE_ROWS
DUMP_ROWS = STAGE_ROWS


CHUNKS_PER_SLAB = MAX_TILE_CHUNKS // N_SLAB
assert CHUNKS_PER_SLAB * N_SLAB == MAX_TILE_CHUNKS


def _some_chunks(chunk_rows, tile, lo, hi, copy_fn):
    for c in range(lo, hi):
        copy_fn(c * ROW_CHUNK, chunk_rows[tile * MAX_TILE_CHUNKS + c])


def _dispatch_kernel(chunk_rows, pad_start, pad_chunks, slot_ref, h2_ref, xs_hbm, stage, zeros, sems, zsem):
    i = pl.program_id(0)
    n_steps = pl.num_programs(0)
    cur = i % 2

    def chunk_copy(slot, stage_row, sorted_row):
        return pltpu.make_async_copy(stage.at[slot, pl.ds(_subrow(stage_row), CHUNK_SUBROWS)],
                                     xs_hbm.at[pl.ds(_subrow(sorted_row), CHUNK_SUBROWS)], sems.at[slot])

    def wait_tile(slot):
        for _ in range(MAX_TILE_CHUNKS):
            chunk_copy(slot, 0, 0).wait()

    @pl.when(i == 0)
    def _():
        zeros[...] = jnp.zeros(zeros.shape, _f32)

        def zero_copy(row):
            return pltpu.make_async_copy(zeros, xs_hbm.at[pl.ds(_subrow(row), CHUNK_SUBROWS)], zsem)

        for e in range(N_EXPERTS + 1):
            def start(c, carry, e=e):
                zero_copy(pad_start[e] + c * ROW_CHUNK).start()
                return carry
            lax.fori_loop(0, pad_chunks[e], start, 0)
        for e in range(N_EXPERTS + 1):
            def wait(c, carry, e=e):
                zero_copy(pad_start[e] + c * ROW_CHUNK).wait()
                return carry
            lax.fori_loop(0, pad_chunks[e], wait, 0)

    slots = slot_ref[...]
    row_i = lax.broadcasted_iota(jnp.int32, (STAGE_ROWS, ROUTE_TILE), 0)
    place = jnp.zeros((STAGE_ROWS, ROUTE_TILE), _f32)
    for kk in range(TOP_K):
        place = place + jnp.where(row_i == slots[kk:kk + 1, :], 1.0, 0.0)
    staged = _dot(place.astype(_bf16), h2_ref[...])

    @pl.when(i >= 2)
    def _():
        wait_tile(cur)

    def send(tile, slot, lo, hi):
        _some_chunks(chunk_rows, tile, lo, hi, lambda sr, dr: chunk_copy(slot, sr, dr).start())

    for j in range(N_SLAB):
        stage.at[cur][pl.ds(j, STAGE_ROWS, stride=N_SLAB), :] = staged[:, j * LANES:(j + 1) * LANES]

        @pl.when(i >= 1)
        def _(j=j):
            send(i - 1, 1 - cur, j * CHUNKS_PER_SLAB, (j + 1) * CHUNKS_PER_SLAB)

    @pl.when(i == n_steps - 1)
    def _():
        @pl.when(i >= 1)
        def _():
            wait_tile(1 - cur)
        send(i, cur, 0, MAX_TILE_CHUNKS)
        wait_tile(cur)


def _dispatch(h2, slots_t, sched, n_rows):
    T = h2.shape[0]
    grid_spec = pltpu.PrefetchScalarGridSpec(
        num_scalar_prefetch=3,
        grid=(T // ROUTE_TILE,),
        in_specs=[pl.BlockSpec((None, SUBLANES, ROUTE_TILE), lambda i, *_: (i, 0, 0)),
                  pl.BlockSpec((ROUTE_TILE, D_MODEL), lambda i, *_: (i, 0))],
        out_specs=pl.BlockSpec(memory_space=pl.ANY),
        scratch_shapes=[pltpu.VMEM((2, STAGE_ROWS * N_SLAB, LANES), _f32),
                        pltpu.VMEM((CHUNK_SUBROWS, LANES), _f32),
                        pltpu.SemaphoreType.DMA((2,)), pltpu.SemaphoreType.DMA(())],
    )
    return pl.pallas_call(
        _dispatch_kernel,
        grid_spec=grid_spec,
        out_shape=jax.ShapeDtypeStruct(((n_rows + DUMP_ROWS) * N_SLAB, LANES), _f32),
        compiler_params=pltpu.CompilerParams(dimension_semantics=("arbitrary",), vmem_limit_bytes=VMEM_LIMIT),
        name="dispatch",
    )(sched["write_rows"], sched["pad_start"], sched["pad_chunks"], slots_t, h2)


GU_GROUP = 2 * LANES


def _expert_kernel(blk_exp, blk_valid, xs_ref, wgu_ref, bgu_ref, wd_ref, bd_ref, y_ref, wgu_s, wd_s):
    i = pl.program_id(0)
    valid = blk_valid[i]

    @pl.when(jnp.logical_or(i == 0, blk_exp[i] != blk_exp[jnp.maximum(i - 1, 0)]))
    def _():
        r_i = lax.broadcasted_iota(jnp.int32, (GU_GROUP, GU_GROUP), 0)
        c_i = lax.broadcasted_iota(jnp.int32, (GU_GROUP, GU_GROUP), 1)
        src = jnp.where(c_i < LANES, 2 * c_i, 2 * (c_i - LANES) + 1)
        perm = jnp.where(r_i == src, 1.0, 0.0).astype(_bf16)
        for b in range(2 * D_FF // GU_GROUP):
            cols = slice(b * GU_GROUP, (b + 1) * GU_GROUP)
            wgu_s[:, cols] = _dot(wgu_ref[:, cols].astype(_bf16), perm).astype(_bf16)
        wd_s[...] = wd_ref[...].astype(_bf16)

    @pl.when(valid > 0)
    def _():
        x = _from_token_major(xs_ref, EXPERT_BLOCK).astype(_bf16)
        y = jnp.zeros((EXPERT_BLOCK, D_MODEL), _f32)
        groups = FF_CHUNK // LANES
        for c in range(D_FF // FF_CHUNK):
            lo = 2 * c * FF_CHUNK
            h = _dot(x, wgu_s[:, lo:lo + 2 * FF_CHUNK]) + bgu_ref[:, lo:lo + 2 * FF_CHUNK]
            hg = jnp.concatenate([h[:, (2 * q) * LANES:(2 * q + 1) * LANES] for q in range(groups)], axis=-1)
            hl = jnp.concatenate([h[:, (2 * q + 1) * LANES:(2 * q + 2) * LANES] for q in range(groups)], axis=-1)
            g = jnp.minimum(hg, SWIGLU_LIMIT)
            lin = jnp.clip(hl, -SWIGLU_LIMIT, SWIGLU_LIMIT)
            act = g * jax.nn.sigmoid(SWIGLU_ALPHA * g) * (lin + 1.0)
            y = y + _dot(act.astype(_bf16), wd_s[c * FF_CHUNK:(c + 1) * FF_CHUNK, :])
        _to_token_major(y_ref, y + bd_ref[...])

    @pl.when(valid == 0)
    def _():
        y_ref[...] = jnp.zeros(y_ref.shape, _f32)


def _experts(layer, xs, n_rows, blk_exp, blk_valid, w_gu, b_gu, w_down, b_down):
    nb = n_rows // EXPERT_BLOCK
    grid_spec = pltpu.PrefetchScalarGridSpec(
        num_scalar_prefetch=2,
        grid=(nb,),
        in_specs=[pl.BlockSpec((EXPERT_BLOCK * N_SLAB, LANES), lambda i, be, bv: (i, 0)),
                  pl.BlockSpec((None, None, D_MODEL, 2 * D_FF), lambda i, be, bv: (layer, be[i], 0, 0)),
                  pl.BlockSpec((None, None, 1, 2 * D_FF), lambda i, be, bv: (layer, be[i], 0, 0)),
                  pl.BlockSpec((None, None, D_FF, D_MODEL), lambda i, be, bv: (layer, be[i], 0, 0)),
                  pl.BlockSpec((None, None, 1, D_MODEL), lambda i, be, bv: (layer, be[i], 0, 0))],
        out_specs=pl.BlockSpec((EXPERT_BLOCK * N_SLAB, LANES), lambda i, be, bv: (i, 0)),
        scratch_shapes=[pltpu.VMEM((D_MODEL, 2 * D_FF), _bf16), pltpu.VMEM((D_FF, D_MODEL), _bf16)],
    )
    return pl.pallas_call(
        _expert_kernel,
        grid_spec=grid_spec,
        out_shape=jax.ShapeDtypeStruct((n_rows * N_SLAB, LANES), _f32),
        compiler_params=pltpu.CompilerParams(dimension_semantics=("arbitrary",), vmem_limit_bytes=VMEM_LIMIT),
        name="experts",
    )(blk_exp, blk_valid, xs, w_gu, b_gu, w_down, b_down)


def _combine_kernel(chunk_rows, x1_ref, route_ref, gate_ref, ys_hbm, out_ref, stage, sems):
    i = pl.program_id(0)
    n_steps = pl.num_programs(0)
    cur = i % 2

    def chunk_copy(slot, stage_row, sorted_row):
        return pltpu.make_async_copy(ys_hbm.at[pl.ds(_subrow(sorted_row), CHUNK_SUBROWS)],
                                     stage.at[slot, pl.ds(_subrow(stage_row), CHUNK_SUBROWS)], sems.at[slot])

    def fetch(tile, slot, lo, hi):
        _some_chunks(chunk_rows, tile, lo, hi, lambda sr, dr: chunk_copy(slot, sr, dr).start())

    def wait_slot(slot):
        for _ in range(MAX_TILE_CHUNKS):
            chunk_copy(slot, 0, 0).wait()

    @pl.when(i == 0)
    def _():
        fetch(0, 0, 0, MAX_TILE_CHUNKS)

    wait_slot(cur)
    nxt = jnp.minimum(i + 1, n_steps - 1)
    slabs = []
    for j in range(N_SLAB):
        slabs.append(stage.at[cur][pl.ds(j, STAGE_ROWS, stride=N_SLAB), :])
        fetch(nxt, 1 - cur, j * CHUNKS_PER_SLAB, (j + 1) * CHUNKS_PER_SLAB)
    rows = jnp.concatenate(slabs, axis=-1).astype(_bf16)
    route = route_ref[...]
    gates = gate_ref[...]
    lane_i = lax.broadcasted_iota(jnp.int32, (ROUTE_TILE, STAGE_ROWS), 1)
    weight = jnp.zeros((ROUTE_TILE, STAGE_ROWS), _f32)
    for kk in range(TOP_K):
        weight = weight + jnp.where(lane_i == route[:, TOP_K + kk:TOP_K + kk + 1], gates[:, kk:kk + 1], 0.0)
    out_ref[...] = x1_ref[...] + _dot(weight.astype(_bf16), rows)

    @pl.when(i == n_steps - 1)
    def _():
        wait_slot(1 - cur)


def _combine(x1, route, rgate, ys, sched):
    T = x1.shape[0]
    tc = ROUTE_TILE
    grid_spec = pltpu.PrefetchScalarGridSpec(
        num_scalar_prefetch=1,
        grid=(T // tc,),
        in_specs=[pl.BlockSpec((tc, D_MODEL), lambda i, *_: (i, 0)),
                  pl.BlockSpec((tc, LANES), lambda i, *_: (i, 0)),
                  pl.BlockSpec((tc, LANES), lambda i, *_: (i, 0)),
                  pl.BlockSpec(memory_space=pl.ANY)],
        out_specs=pl.BlockSpec((tc, D_MODEL), lambda i, *_: (i, 0)),
        scratch_shapes=[pltpu.VMEM((2, STAGE_ROWS * N_SLAB, LANES), _f32),
                        pltpu.SemaphoreType.DMA((2,))],
    )
    return pl.pallas_call(
        _combine_kernel,
        grid_spec=grid_spec,
        out_shape=jax.ShapeDtypeStruct((T, D_MODEL), _f32),
        compiler_params=pltpu.CompilerParams(dimension_semantics=("arbitrary",), vmem_limit_bytes=VMEM_LIMIT),
        name="combine",
    )(sched["read_rows"], x1, route, rgate, ys)


def _prepare_params(w):
    L = w["w_in"].shape[0]
    p = {}
    row = lambda a: a.reshape(L, 1, a.shape[-1])
    p["attn_norm_g"] = row(w["attn_norm_g"])
    p["w_in"] = jnp.pad(w["w_in"], ((0, 0), (0, 0), (0, D_IN_PAD - w["w_in"].shape[-1]))).astype(_bf16)
    eye = jnp.eye(len(POOL_WINDOWS), dtype=_f32)
    p["pool_w"] = jnp.einsum("lgcd,gh->lgchd", w["pool_w"], eye).reshape(L, POOL_W, POOL_W).astype(_bf16)
    p["pool_scale"] = row(w["pool_scale"])
    p["conv_dw"] = w["conv_dw"]
    p["conv_dw_b"] = row(w["conv_dw_b"])
    p["conv_ln_g"] = row(w["conv_ln_g"])
    p["conv_ln_b"] = row(w["conv_ln_b"])
    p["conv_pw"] = w["conv_pw"].astype(_bf16)
    p["conv_pw_b"] = row(w["conv_pw_b"])
    p["q_a_g"] = row(w["q_a_g"])
    wuq = w["w_uq"].reshape(L, Q_LORA, N_HEADS, QK_HEAD)
    p["w_uq"] = jnp.pad(wuq, ((0, 0), (0, 0), (0, 0), (0, HEAD_PAD - QK_HEAD))).reshape(
        L, Q_LORA, N_HEADS * HEAD_PAD).astype(_bf16)
    p["kv_a_g"] = row(w["kv_a_g"])
    wukv = w["w_ukv"].reshape(L, KV_LORA, N_HEADS, QK_NOPE + V_DIM)
    p["w_ukv"] = jnp.concatenate([wukv[..., :QK_NOPE].reshape(L, KV_LORA, N_HEADS * QK_NOPE),
                                  wukv[..., QK_NOPE:].reshape(L, KV_LORA, N_HEADS * V_DIM)], axis=-1).astype(_bf16)
    pad_head = lambda g: jnp.pad(g, ((0, 0), (0, HEAD_PAD - QK_HEAD))).reshape(L, 1, HEAD_PAD)
    p["q_norm_g"] = pad_head(w["q_norm_g"])
    p["k_norm_g"] = pad_head(w["k_norm_g"])
    p["pool_out_g"] = row(w["pool_out_g"])
    p["conv_out_g"] = row(w["conv_out_g"])
    p["attn_out_g"] = row(w["attn_out_g"])
    p["w_o"] = w["w_o"].astype(_bf16)
    p["ffn_norm_g"] = row(w["ffn_norm_g"])
    p["w_router"] = jnp.pad(w["w_router"], ((0, 0), (0, 0), (0, LANES - N_EXPERTS))).astype(_bf16)
    p["b_router"] = jnp.pad(w["b_router"], ((0, 0), (0, LANES - N_EXPERTS)),
                            constant_values=NEG_BIG).reshape(L, 1, LANES)
    E = w["w_gu"].shape[1]
    p["w_gu"] = w["w_gu"]
    bgu = w["b_gu"].reshape(L, E, 2 * D_FF // GU_GROUP, LANES, 2)
    p["b_gu"] = jnp.swapaxes(bgu, -1, -2).reshape(L, E, 1, 2 * D_FF)
    p["w_down"] = w["w_down"]
    p["b_down"] = w["b_down"].reshape(L, E, 1, D_MODEL)
    return p


def _rope_tables(positions):
    inv_freq = 1.0 / (ROPE_THETA ** (jnp.arange(0, QK_ROPE, 2, dtype=_f32) / QK_ROPE))
    ang = positions.astype(_f32)[..., None] * inv_freq
    cos, sin = jnp.cos(ang), jnp.sin(ang)
    zeros = jnp.zeros(cos.shape[:-1] + (LANES - QK_ROPE,), _f32)
    return (jnp.concatenate([cos, cos, zeros], axis=-1), jnp.concatenate([-sin, sin, zeros], axis=-1))


def _routing_schedule(cnt_f, n_tokens):
    n = cnt_f[:, 0:N_EXPERTS].astype(jnp.int32)
    counts = jnp.sum(n, axis=0)
    spare = ROW_CHUNK - 1
    padded = jnp.where(counts > 0, (counts + spare + EXPERT_BLOCK - 1) // EXPERT_BLOCK * EXPERT_BLOCK, 0)
    pend = jnp.cumsum(padded)
    pstart = pend - padded
    base = pstart[None, :] + jnp.cumsum(n, axis=0) - n
    n_chunks = (n + ROW_CHUNK - 1) // ROW_CHUNK
    max_rows = n_tokens * TOP_K + N_EXPERTS * (spare + EXPERT_BLOCK - 1) + ROW_CHUNK
    n_blocks = -(-max_rows // EXPERT_BLOCK)
    blk_row = jnp.arange(n_blocks, dtype=jnp.int32) * EXPERT_BLOCK
    blk_exp = jnp.minimum(jnp.sum((blk_row[:, None] >= pend[None, :]).astype(jnp.int32), axis=1), N_EXPERTS - 1)
    onehot = (blk_exp[:, None] == jnp.arange(N_EXPERTS, dtype=jnp.int32)[None, :]).astype(jnp.int32)
    seg_end = jnp.sum(onehot * (pstart + counts)[None, :], axis=1)
    blk_valid = jnp.where(blk_row < pend[-1], jnp.clip(seg_end - blk_row, 0, EXPERT_BLOCK), 0)
    i32 = lambda a: a.astype(jnp.int32)
    n_rows = n_blocks * EXPERT_BLOCK
    seg_chunks = (padded - counts + ROW_CHUNK - 1) // ROW_CHUNK
    pad_start = jnp.concatenate([pend - seg_chunks * ROW_CHUNK, pend[-1:]])
    pad_chunks = jnp.concatenate([seg_chunks, (n_rows + DUMP_ROWS - pend[-1:]) // ROW_CHUNK])
    cum_end = jnp.cumsum(n_chunks, axis=1)
    cum_start = cum_end - n_chunks
    c_idx = jnp.arange(MAX_TILE_CHUNKS, dtype=jnp.int32)
    e_of = jnp.minimum(jnp.sum((c_idx[None, :, None] >= cum_end[:, None, :]).astype(jnp.int32), axis=-1),
                       N_EXPERTS - 1)
    pick = (e_of[:, :, None] == jnp.arange(N_EXPERTS, dtype=jnp.int32)[None, None, :]).astype(jnp.int32)
    chunk_dst = (jnp.sum(pick * base[:, None, :], axis=-1)
                 + (c_idx[None, :] - jnp.sum(pick * cum_start[:, None, :], axis=-1)) * ROW_CHUNK)
    needed = c_idx[None, :] < cum_end[:, -1:]
    write_rows = jnp.where(needed, chunk_dst, n_rows + c_idx[None, :] * ROW_CHUNK)
    read_rows = jnp.where(needed, chunk_dst, 0)
    sched = dict(write_rows=i32(write_rows).reshape(-1), read_rows=i32(read_rows).reshape(-1),
                 pad_start=i32(pad_start), pad_chunks=i32(pad_chunks))
    return sched, i32(blk_exp), i32(blk_valid), n_rows


def kernel(x, positions, attn_norm_g, w_in, pool_w, pool_scale, conv_dw, conv_dw_b, conv_ln_g, conv_ln_b, conv_pw, conv_pw_b, q_a_g, w_uq, kv_a_g, w_ukv, q_norm_g, k_norm_g, pool_out_g, conv_out_g, attn_out_g, w_o, ffn_norm_g, w_router, b_router, w_gu, b_gu, w_down, b_down):
    B, S, D = x.shape
    T = B * S
    depth = w_in.shape[0]
    p = _prepare_params(dict(
        attn_norm_g=attn_norm_g, w_in=w_in, pool_w=pool_w, pool_scale=pool_scale, conv_dw=conv_dw,
        conv_dw_b=conv_dw_b, conv_ln_g=conv_ln_g, conv_ln_b=conv_ln_b, conv_pw=conv_pw, conv_pw_b=conv_pw_b,
        q_a_g=q_a_g, w_uq=w_uq, kv_a_g=kv_a_g, w_ukv=w_ukv, q_norm_g=q_norm_g, k_norm_g=k_norm_g,
        pool_out_g=pool_out_g, conv_out_g=conv_out_g, attn_out_g=attn_out_g, w_o=w_o, ffn_norm_g=ffn_norm_g,
        w_router=w_router, b_router=b_router, w_gu=w_gu, b_gu=b_gu, w_down=w_down, b_down=b_down))
    rope_c, rope_s = _rope_tables(positions)
    for layer in range(depth):
        mix, q, k, v = _mixer_front(layer, x, rope_c, rope_s, p)
        att = _attention(q, k, v)
        x1, h2, route, rgate, cnt = _post_mix(layer, x.reshape(T, D), mix.reshape(T, -1), att.reshape(T, -1), p)
        sched, blk_exp, blk_valid, n_rows = _routing_schedule(cnt, T)
        slots_t = route[:, TOP_K:2 * TOP_K].reshape(T // ROUTE_TILE, ROUTE_TILE, TOP_K).transpose(0, 2, 1)
        slots_t = jnp.pad(slots_t, ((0, 0), (0, SUBLANES - TOP_K), (0, 0)), constant_values=-1)
        xs = _dispatch(h2, slots_t, sched, n_rows)
        ys = _experts(layer, xs, n_rows, blk_exp, blk_valid, p["w_gu"], p["b_gu"], p["w_down"], p["b_down"])
        x = _combine(x1, route, rgate, ys, sched).reshape(B, S, D)
    return x
```

```python
import functools
import math

import jax
import jax.numpy as jnp
from jax import lax
from jax.experimental import pallas as pl
from jax.experimental.pallas import tpu as pltpu

D_MODEL = 1024
CHUNK = 64
POOL_W = 256
POOL_WINDOWS = (2, 4, 8, 16)
POOL_GC = 64
CONV_W = 256
CONV_K = 31
N_HEADS = 4
QK_NOPE = 128
QK_ROPE = 64
QK_HEAD = QK_NOPE + QK_ROPE
V_DIM = 128
Q_LORA = 256
KV_LORA = 128
ATTN_W = N_HEADS * V_DIM
ROPE_THETA = 10000.0
N_EXPERTS = 32
TOP_K = 4
D_FF = D_MODEL
SWIGLU_LIMIT = 7.0
SWIGLU_ALPHA = 1.702
EPS = 1e-6

LANES = 128
HEAD_PAD = 2 * LANES
D_IN_PAD = 1280
POOL_HALO = 16
CONV_HALO = 32
NEG_BIG = -1e30

TM_FRONT = 512
TQ = 512
TM_POST = 1024
ROUTE_TILE = 128
ROW_CHUNK = 8
SUBLANES = 8
STAGE_ROWS = -(-(ROUTE_TILE * TOP_K + N_EXPERTS * (ROW_CHUNK - 1)) // LANES) * LANES
EXPERT_BLOCK = 512
FF_CHUNK = 256
VMEM_LIMIT = 56 * 1024 * 1024

_bf16 = jnp.bfloat16
_f32 = jnp.float32


def _dot(a, b):
    return jnp.dot(a, b, preferred_element_type=_f32)


def _rms(x, width):
    return lax.rsqrt(jnp.sum(x * x, axis=-1, keepdims=True) * (1.0 / width) + EPS)


def _layer_spec(shape, layer, n_grid):
    nd = len(shape)
    block = (None,) + tuple(shape[1:])
    if n_grid == 1:
        return pl.BlockSpec(block, lambda i: (layer,) + (0,) * (nd - 1))
    return pl.BlockSpec(block, lambda i, j: (layer,) + (0,) * (nd - 1))


def _rope(x, c_tab, s_tab):
    lane = lax.broadcasted_iota(jnp.int32, x.shape, 1)
    partner = jnp.where(lane < QK_ROPE // 2, pltpu.roll(x, LANES - QK_ROPE // 2, 1),
                        pltpu.roll(x, QK_ROPE // 2, 1))
    return x * c_tab + partner * s_tab


def _front_kernel(x_ref, ropec_ref, ropes_ref, ang_ref, win_ref, poolw_ref, pscale_ref, cdw_ref, cdwb_ref,
                  clng_ref, clnb_ref, cpw_ref, cpwb_ref, qag_ref, wuq_ref, kvag_ref, wukv_ref, qng_ref,
                  kng_ref, poutg_ref, coutg_ref,
                  mix_ref, q_ref, k_ref, v_ref, pool_ext, conv_ext):
    j = pl.program_id(1)
    tm = x_ref.shape[0]

    x = x_ref[...]
    h = x * _rms(x, D_MODEL) * ang_ref[...]
    z = _dot(h.astype(_bf16), win_ref[...])

    u = z[:, 0:POOL_W]

    @pl.when(j == 0)
    def _():
        pool_ext[0:POOL_HALO, :] = jnp.zeros((POOL_HALO, POOL_W), _f32)
        conv_ext[0:CONV_HALO, :] = jnp.zeros((CONV_HALO, CONV_W), _f32)

    @pl.when(j > 0)
    def _():
        pool_ext[0:POOL_HALO, :] = pool_ext[tm:tm + POOL_HALO, :]
        conv_ext[0:CONV_HALO, :] = conv_ext[tm:tm + CONV_HALO, :]

    pool_ext[POOL_HALO:POOL_HALO + tm, :] = u
    t_pos = (j * tm + lax.broadcasted_iota(jnp.int32, (tm, LANES), 0) + 1).astype(_f32)
    lane = lax.broadcasted_iota(jnp.int32, (tm, LANES), 1)
    pooled_halves = []
    for half, (w_lo, w_hi) in enumerate(((POOL_WINDOWS[0], POOL_WINDOWS[1]),
                                         (POOL_WINDOWS[2], POOL_WINDOWS[3]))):
        cols = slice(half * LANES, (half + 1) * LANES)
        s_lo = pool_ext[POOL_HALO:POOL_HALO + tm, cols]
        for d in range(1, w_lo):
            s_lo = s_lo + pool_ext[POOL_HALO - d:POOL_HALO - d + tm, cols]
        s_hi = s_lo
        for d in range(w_lo, w_hi):
            s_hi = s_hi + pool_ext[POOL_HALO - d:POOL_HALO - d + tm, cols]
        first = lane < POOL_GC
        win_sum = jnp.where(first, s_lo, s_hi)
        cnt = jnp.minimum(t_pos, jnp.where(first, float(w_lo), float(w_hi)))
        pooled_halves.append(win_sum / cnt - u[:, cols])
    pooled = jnp.concatenate(pooled_halves, axis=-1)
    y_pool = _dot(pooled.astype(_bf16), poolw_ref[...]) * pscale_ref[...]
    y_pool = y_pool * _rms(y_pool, POOL_W) * poutg_ref[...]

    a = z[:, POOL_W:POOL_W + CONV_W]
    gate = z[:, POOL_W + CONV_W:POOL_W + 2 * CONV_W]
    conv_ext[CONV_HALO:CONV_HALO + tm, :] = a * jax.nn.sigmoid(gate)
    base = CONV_HALO - (CONV_K - 1)
    acc = cdwb_ref[...]
    window = conv_ext[...]
    n_win = window.shape[0]
    for r in range(SUBLANES):
        tiles = [(base + kk) // SUBLANES for kk in range(CONV_K) if (base + kk) % SUBLANES == r]
        shifted = window if r == 0 else pltpu.roll(window, n_win - r, 0)
        for m in tiles:
            kk = m * SUBLANES + r - base
            acc = acc + shifted[m * SUBLANES:m * SUBLANES + tm, :] * cdw_ref[kk:kk + 1, :]
    mu = jnp.mean(acc, axis=-1, keepdims=True)
    cen = acc - mu
    var = jnp.mean(cen * cen, axis=-1, keepdims=True)
    ln = cen * lax.rsqrt(var + EPS) * clng_ref[...] + clnb_ref[...]
    sw = ln * jax.nn.sigmoid(ln)
    y_conv = _dot(sw.astype(_bf16), cpw_ref[...]) + cpwb_ref[...]
    y_conv = y_conv * _rms(y_conv, CONV_W) * coutg_ref[...]

    mix_ref[:, 0:POOL_W] = y_pool.astype(_bf16)
    mix_ref[:, POOL_W:POOL_W + CONV_W] = y_conv.astype(_bf16)

    s1 = POOL_W + 2 * CONV_W
    c_q = z[:, s1:s1 + Q_LORA]
    c_kv = z[:, s1 + Q_LORA:s1 + Q_LORA + KV_LORA]
    k_rope = z[:, s1 + Q_LORA + KV_LORA:s1 + Q_LORA + KV_LORA + LANES]
    c_tab = ropec_ref[...]
    s_tab = ropes_ref[...]

    q_all = _dot((c_q * _rms(c_q, Q_LORA) * qag_ref[...]).astype(_bf16), wuq_ref[...])
    kv_all = _dot((c_kv * _rms(c_kv, KV_LORA) * kvag_ref[...]).astype(_bf16), wukv_ref[...])
    sm_scale = math.log2(math.e) / math.sqrt(QK_HEAD)
    kr_ss = jnp.sum(k_rope * k_rope, axis=-1, keepdims=True)
    kr_rot = _rope(k_rope * kng_ref[:, LANES:2 * LANES], c_tab, s_tab)
    for hd in range(N_HEADS):
        qh = q_all[:, hd * HEAD_PAD:(hd + 1) * HEAD_PAD]
        qh = qh * (_rms(qh, QK_HEAD) * sm_scale) * qng_ref[...]
        q_ref[:, hd * HEAD_PAD:hd * HEAD_PAD + LANES] = qh[:, 0:LANES].astype(_bf16)
        q_ref[:, hd * HEAD_PAD + LANES:(hd + 1) * HEAD_PAD] = _rope(qh[:, LANES:], c_tab, s_tab).astype(_bf16)
        kn = kv_all[:, hd * QK_NOPE:(hd + 1) * QK_NOPE]
        k_rs = lax.rsqrt((jnp.sum(kn * kn, axis=-1, keepdims=True) + kr_ss) * (1.0 / QK_HEAD) + EPS)
        k_ref[:, hd * HEAD_PAD:hd * HEAD_PAD + LANES] = (kn * k_rs * kng_ref[:, 0:LANES]).astype(_bf16)
        k_ref[:, hd * HEAD_PAD + LANES:(hd + 1) * HEAD_PAD] = (kr_rot * k_rs).astype(_bf16)
        v_lo = hd * (V_DIM + LANES)
        v_ref[:, v_lo:v_lo + V_DIM] = kv_all[:, (N_HEADS + hd) * V_DIM:(N_HEADS + hd + 1) * V_DIM].astype(_bf16)
        v_ref[:, v_lo + V_DIM:v_lo + V_DIM + LANES] = jnp.ones((tm, LANES), _bf16)


def _mixer_front(layer, x, rope_c, rope_s, p):
    B, S, _ = x.shape
    tm = TM_FRONT
    row = lambda w: pl.BlockSpec((None, tm, w), lambda b, j: (b, j, 0))
    names = ("attn_norm_g", "w_in", "pool_w", "pool_scale", "conv_dw", "conv_dw_b", "conv_ln_g", "conv_ln_b",
             "conv_pw", "conv_pw_b", "q_a_g", "w_uq", "kv_a_g", "w_ukv", "q_norm_g", "k_norm_g",
             "pool_out_g", "conv_out_g")
    weights = [p[n] for n in names]
    in_specs = [row(D_MODEL), row(LANES), row(LANES)] + [_layer_spec(w.shape, layer, 2) for w in weights]
    out_shape = (jax.ShapeDtypeStruct((B, S, POOL_W + CONV_W), _bf16),
                 jax.ShapeDtypeStruct((B, S, N_HEADS * HEAD_PAD), _bf16),
                 jax.ShapeDtypeStruct((B, S, N_HEADS * HEAD_PAD), _bf16),
                 jax.ShapeDtypeStruct((B, S, N_HEADS * (V_DIM + LANES)), _bf16))
    out_specs = (row(POOL_W + CONV_W), row(N_HEADS * HEAD_PAD), row(N_HEADS * HEAD_PAD),
                 row(N_HEADS * (V_DIM + LANES)))
    return pl.pallas_call(
        _front_kernel,
        grid=(B, S // tm),
        in_specs=in_specs,
        out_specs=out_specs,
        out_shape=out_shape,
        scratch_shapes=[pltpu.VMEM((tm + POOL_HALO, POOL_W), _f32),
                        pltpu.VMEM((tm + CONV_HALO, CONV_W), _f32)],
        compiler_params=pltpu.CompilerParams(dimension_semantics=("arbitrary", "arbitrary"),
                                             vmem_limit_bytes=VMEM_LIMIT),
        name="mixer_front",
    )(x, rope_c, rope_s, *weights)


def _attn_kernel(q_ref, k_ref, v_ref, o_ref):
    qi = pl.program_id(2)
    tq = q_ref.shape[0]
    q_chunk = lax.broadcasted_iota(jnp.int32, (tq, tq), 0) // CHUNK
    k_chunk = lax.broadcasted_iota(jnp.int32, (tq, tq), 1) // CHUNK

    def tile_body(n):
        keys = (n + 1) * tq
        s = lax.dot_general(q_ref[...], k_ref[0:keys, :], (((1,), (1,)), ((), ())),
                            preferred_element_type=_f32)
        diag = jnp.where(k_chunk <= q_chunk, s[:, n * tq:], NEG_BIG)
        s = diag if n == 0 else jnp.concatenate([s[:, :n * tq], diag], axis=-1)
        p = jnp.exp2(s - jnp.max(s, axis=-1, keepdims=True)).astype(_bf16)
        acc = _dot(p, v_ref[0:keys, :])
        o_ref[...] = (acc[:, 0:V_DIM] / acc[:, V_DIM:V_DIM + 1]).astype(o_ref.dtype)

    for n in range(k_ref.shape[0] // tq):
        pl.when(qi == n)(functools.partial(tile_body, n))


def _attention(q, k, v):
    B, S, _ = q.shape
    nq = S // TQ
    return pl.pallas_call(
        _attn_kernel,
        grid=(B, N_HEADS, nq),
        in_specs=[pl.BlockSpec((None, TQ, HEAD_PAD), lambda b, h, i: (b, i, h)),
                  pl.BlockSpec((None, S, HEAD_PAD), lambda b, h, i: (b, 0, h)),
                  pl.BlockSpec((None, S, V_DIM + LANES), lambda b, h, i: (b, 0, h))],
        out_specs=pl.BlockSpec((None, TQ, V_DIM), lambda b, h, i: (b, i, h)),
        out_shape=jax.ShapeDtypeStruct((B, S, ATTN_W), _bf16),
        compiler_params=pltpu.CompilerParams(dimension_semantics=("arbitrary", "arbitrary", "arbitrary"),
                                             vmem_limit_bytes=VMEM_LIMIT),
        name="attention",
    )(q, k, v)


def _post_kernel(x_ref, mix_ref, att_ref, aog_ref, wo_ref, fng_ref, wr_ref, br_ref,
                 x1_ref, h2_ref, route_ref, rgate_ref, cnt_ref):
    tm = x_ref.shape[0]
    att = att_ref[...].astype(_f32)
    att_n = (att * _rms(att, ATTN_W) * aog_ref[...]).astype(_bf16)
    x1 = x_ref[...] + _dot(mix_ref[...], wo_ref[0:POOL_W + CONV_W, :]) + _dot(att_n, wo_ref[POOL_W + CONV_W:, :])
    x1_ref[...] = x1
    h2 = (x1 * _rms(x1, D_MODEL) * fng_ref[...]).astype(_bf16)
    h2_ref[...] = h2

    logits = _dot(h2, wr_ref[...]) + br_ref[...]
    lane = lax.broadcasted_iota(jnp.int32, logits.shape, 1)
    lane_f = lane.astype(_f32)
    work = logits
    sel = jnp.zeros(logits.shape, _f32)
    onehots, vals, idxs = [], [], []
    for _ in range(TOP_K):
        mx = jnp.max(work, axis=-1, keepdims=True)
        idx_f = jnp.min(jnp.where(work == mx, lane_f, float(LANES)), axis=-1, keepdims=True)
        idx = idx_f.astype(jnp.int32)
        hot = lane == idx
        onehots.append(hot)
        vals.append(mx)
        idxs.append(idx)
        sel = jnp.where(hot, 1.0, sel)
        work = jnp.where(hot, -jnp.inf, work)
    exps = [jnp.exp(vk - vals[0]) for vk in vals]
    denom = exps[0] + exps[1] + exps[2] + exps[3]

    nt = tm // ROUTE_TILE
    r_i = lax.broadcasted_iota(jnp.int32, (ROUTE_TILE, ROUTE_TILE), 0)
    c_i = lax.broadcasted_iota(jnp.int32, (ROUTE_TILE, ROUTE_TILE), 1)
    tri = jnp.where(c_i < r_i, 1.0, 0.0).astype(_bf16)
    upper = jnp.where(r_i < c_i, 1.0, 0.0).astype(_bf16)
    sel_b = sel.astype(_bf16)
    ranks, cnts = [], []
    for s in range(nt):
        rows = slice(s * ROUTE_TILE, (s + 1) * ROUTE_TILE)
        ranks.append(_dot(tri, sel_b[rows, :]))
        cnts.append(jnp.sum(sel[rows, :], axis=0, keepdims=True))
    cnt = jnp.concatenate(cnts, axis=0)
    cnt_ref[...] = cnt
    chunks = jnp.floor((cnt + (ROW_CHUNK - 1)) * (1.0 / ROW_CHUNK))
    boff = _dot(chunks.astype(_bf16), upper) * float(ROW_CHUNK)
    slot = jnp.concatenate(
        [ranks[s] + jnp.broadcast_to(boff[s:s + 1, :], (ROUTE_TILE, LANES)) for s in range(nt)], axis=0)

    route = jnp.zeros(logits.shape, jnp.int32)
    rgate = jnp.zeros(logits.shape, _f32)
    for kk in range(TOP_K):
        col = jnp.sum(jnp.where(onehots[kk], slot, 0.0), axis=-1, keepdims=True).astype(jnp.int32)
        route = jnp.where(lane == kk, idxs[kk], route)
        route = jnp.where(lane == TOP_K + kk, col, route)
        rgate = jnp.where(lane == kk, exps[kk] / denom, rgate)
    route_ref[...] = route
    rgate_ref[...] = rgate


def _post_mix(layer, x, mix, att, p):
    T = x.shape[0]
    tm = TM_POST
    row = lambda w: pl.BlockSpec((tm, w), lambda i: (i, 0))
    names = ("attn_out_g", "w_o", "ffn_norm_g", "w_router", "b_router")
    weights = [p[n] for n in names]
    out_shape = (jax.ShapeDtypeStruct((T, D_MODEL), _f32), jax.ShapeDtypeStruct((T, D_MODEL), _bf16),
                 jax.ShapeDtypeStruct((T, LANES), jnp.int32), jax.ShapeDtypeStruct((T, LANES), _f32),
                 jax.ShapeDtypeStruct((T // ROUTE_TILE, LANES), _f32))
    return pl.pallas_call(
        _post_kernel,
        grid=(T // tm,),
        in_specs=[row(D_MODEL), row(POOL_W + CONV_W), row(ATTN_W)] + [_layer_spec(w.shape, layer, 1) for w in weights],
        out_specs=(row(D_MODEL), row(D_MODEL), row(LANES), row(LANES),
                   pl.BlockSpec((tm // ROUTE_TILE, LANES), lambda i: (i, 0))),
        out_shape=out_shape,
        compiler_params=pltpu.CompilerParams(dimension_semantics=("arbitrary",), vmem_limit_bytes=VMEM_LIMIT),
        name="post_mix",
    )(x, mix, att, *weights)


N_SLAB = D_MODEL // LANES
assert N_SLAB == SUBLANES
CHUNK_SUBROWS = ROW_CHUNK * N_SLAB


def _to_token_major(ref, value):
    rows = value.shape[0]
    for j in range(N_SLAB):
        ref[pl.ds(j, rows, stride=N_SLAB), :] = value[:, j * LANES:(j + 1) * LANES]


def _from_token_major(ref, rows):
    return jnp.concatenate([ref[pl.ds(j, rows, stride=N_SLAB), :] for j in range(N_SLAB)], axis=-1)


def _subrow(row):
    return pl.multiple_of(row * N_SLAB, N_SLAB)


MAX_TILE_CHUNKS = ROUTE_TILE * TOP_K // ROW_CHUNK + N_EXPERTS
assert MAX_TILE_CHUNKS * ROW_CHUNK == STAGE_ROWS
DUMP_ROWS = STAGE_ROWS


ALWAYS_CHUNKS = ROUTE_TILE * TOP_K // ROW_CHUNK + N_EXPERTS // 4
CHUNK_GROUP = 8
assert (MAX_TILE_CHUNKS - ALWAYS_CHUNKS) % CHUNK_GROUP == 0


def _tile_chunks(chunk_rows, n_total, tile, chunk_fn):
    def run(lo, hi):
        for c in range(lo, hi):
            chunk_fn(c, c * ROW_CHUNK, chunk_rows[tile * MAX_TILE_CHUNKS + c])

    run(0, ALWAYS_CHUNKS)
    for lo in range(ALWAYS_CHUNKS, MAX_TILE_CHUNKS, CHUNK_GROUP):
        pl.when(n_total[tile] > lo)(functools.partial(run, lo, lo + CHUNK_GROUP))


def _dispatch_kernel(chunk_rows, n_total, pad_start, pad_chunks,
                     slot_ref, h2_ref, xs_hbm, stage, zeros, sems, zsem):
    i = pl.program_id(0)
    n_steps = pl.num_programs(0)
    cur = i % 2

    def chunk_copy(slot, stage_row, sorted_row):
        return pltpu.make_async_copy(stage.at[slot, pl.ds(_subrow(stage_row), CHUNK_SUBROWS)],
                                     xs_hbm.at[pl.ds(_subrow(sorted_row), CHUNK_SUBROWS)], sems.at[slot])

    def send_tile(tile, slot):
        _tile_chunks(chunk_rows, n_total, tile, lambda c, sr, dr: chunk_copy(slot, sr, dr).start(priority=c % 2))

    def wait_tile(tile, slot):
        _tile_chunks(chunk_rows, n_total, tile, lambda c, sr, dr: chunk_copy(slot, 0, 0).wait())

    @pl.when(i == 0)
    def _():
        zeros[...] = jnp.zeros(zeros.shape, _f32)

        def zero_copy(row):
            return pltpu.make_async_copy(zeros, xs_hbm.at[pl.ds(_subrow(row), CHUNK_SUBROWS)], zsem)

        for e in range(N_EXPERTS + 1):
            def start(c, carry, e=e):
                zero_copy(pad_start[e] + c * ROW_CHUNK).start()
                return carry
            lax.fori_loop(0, pad_chunks[e], start, 0)
        for e in range(N_EXPERTS + 1):
            def wait(c, carry, e=e):
                zero_copy(pad_start[e] + c * ROW_CHUNK).wait()
                return carry
            lax.fori_loop(0, pad_chunks[e], wait, 0)

    slots = slot_ref[...]
    row_i = lax.broadcasted_iota(jnp.int32, (STAGE_ROWS, ROUTE_TILE), 0)
    place = jnp.zeros((STAGE_ROWS, ROUTE_TILE), _f32)
    for kk in range(TOP_K):
        place = place + jnp.where(row_i == slots[kk:kk + 1, :], 1.0, 0.0)
    staged = _dot(place.astype(_bf16), h2_ref[...])
    _to_token_major(stage.at[cur], staged)

    @pl.when(i > 0)
    def _():
        wait_tile(i - 1, 1 - cur)

    send_tile(i, cur)

    @pl.when(i == n_steps - 1)
    def _():
        wait_tile(i, cur)


def _dispatch(h2, slots_t, sched, n_rows):
    T = h2.shape[0]
    grid_spec = pltpu.PrefetchScalarGridSpec(
        num_scalar_prefetch=4,
        grid=(T // ROUTE_TILE,),
        in_specs=[pl.BlockSpec((None, SUBLANES, ROUTE_TILE), lambda i, *_: (i, 0, 0)),
                  pl.BlockSpec((ROUTE_TILE, D_MODEL), lambda i, *_: (i, 0))],
        out_specs=pl.BlockSpec(memory_space=pl.ANY),
        scratch_shapes=[pltpu.VMEM((2, STAGE_ROWS * N_SLAB, LANES), _f32),
                        pltpu.VMEM((CHUNK_SUBROWS, LANES), _f32),
                        pltpu.SemaphoreType.DMA((2,)), pltpu.SemaphoreType.DMA(())],
    )
    return pl.pallas_call(
        _dispatch_kernel,
        grid_spec=grid_spec,
        out_shape=jax.ShapeDtypeStruct(((n_rows + DUMP_ROWS) * N_SLAB, LANES), _f32),
        compiler_params=pltpu.CompilerParams(dimension_semantics=("arbitrary",), vmem_limit_bytes=VMEM_LIMIT),
        name="dispatch",
    )(sched["write_rows"], sched["n_total"], sched["pad_start"], sched["pad_chunks"], slots_t, h2)


GU_GROUP = 2 * LANES


def _expert_kernel(blk_exp, blk_valid, xs_ref, wgu_ref, bgu_ref, wd_ref, bd_ref, y_ref, wgu_s, wd_s):
    i = pl.program_id(0)
    valid = blk_valid[i]

    @pl.when(jnp.logical_or(i == 0, blk_exp[i] != blk_exp[jnp.maximum(i - 1, 0)]))
    def _():
        r_i = lax.broadcasted_iota(jnp.int32, (GU_GROUP, GU_GROUP), 0)
        c_i = lax.broadcasted_iota(jnp.int32, (GU_GROUP, GU_GROUP), 1)
        src = jnp.where(c_i < LANES, 2 * c_i, 2 * (c_i - LANES) + 1)
        perm = jnp.where(r_i == src, 1.0, 0.0).astype(_bf16)
        for b in range(2 * D_FF // GU_GROUP):
            cols = slice(b * GU_GROUP, (b + 1) * GU_GROUP)
            wgu_s[:, cols] = _dot(wgu_ref[:, cols].astype(_bf16), perm).astype(_bf16)
        wd_s[...] = wd_ref[...].astype(_bf16)

    @pl.when(valid > 0)
    def _():
        x = _from_token_major(xs_ref, EXPERT_BLOCK).astype(_bf16)
        y = jnp.zeros((EXPERT_BLOCK, D_MODEL), _f32)
        groups = FF_CHUNK // LANES
        for c in range(D_FF // FF_CHUNK):
            lo = 2 * c * FF_CHUNK
            h = _dot(x, wgu_s[:, lo:lo + 2 * FF_CHUNK]) + bgu_ref[:, lo:lo + 2 * FF_CHUNK]
            hg = jnp.concatenate([h[:, (2 * q) * LANES:(2 * q + 1) * LANES] for q in range(groups)], axis=-1)
            hl = jnp.concatenate([h[:, (2 * q + 1) * LANES:(2 * q + 2) * LANES] for q in range(groups)], axis=-1)
            g = jnp.minimum(hg, SWIGLU_LIMIT)
            lin = jnp.clip(hl, -SWIGLU_LIMIT, SWIGLU_LIMIT)
            act = g * jax.nn.sigmoid(SWIGLU_ALPHA * g) * (lin + 1.0)
            y = y + _dot(act.astype(_bf16), wd_s[c * FF_CHUNK:(c + 1) * FF_CHUNK, :])
        _to_token_major(y_ref, y + bd_ref[...])

    @pl.when(valid == 0)
    def _():
        y_ref[...] = jnp.zeros(y_ref.shape, _f32)


def _experts(layer, xs, n_rows, blk_exp, blk_valid, w_gu, b_gu, w_down, b_down):
    nb = n_rows // EXPERT_BLOCK
    grid_spec = pltpu.PrefetchScalarGridSpec(
        num_scalar_prefetch=2,
        grid=(nb,),
        in_specs=[pl.BlockSpec((EXPERT_BLOCK * N_SLAB, LANES), lambda i, be, bv: (i, 0)),
                  pl.BlockSpec((None, None, D_MODEL, 2 * D_FF), lambda i, be, bv: (layer, be[i], 0, 0)),
                  pl.BlockSpec((None, None, 1, 2 * D_FF), lambda i, be, bv: (layer, be[i], 0, 0)),
                  pl.BlockSpec((None, None, D_FF, D_MODEL), lambda i, be, bv: (layer, be[i], 0, 0)),
                  pl.BlockSpec((None, None, 1, D_MODEL), lambda i, be, bv: (layer, be[i], 0, 0))],
        out_specs=pl.BlockSpec((EXPERT_BLOCK * N_SLAB, LANES), lambda i, be, bv: (i, 0)),
        scratch_shapes=[pltpu.VMEM((D_MODEL, 2 * D_FF), _bf16), pltpu.VMEM((D_FF, D_MODEL), _bf16)],
    )
    return pl.pallas_call(
        _expert_kernel,
        grid_spec=grid_spec,
        out_shape=jax.ShapeDtypeStruct((n_rows * N_SLAB, LANES), _f32),
        compiler_params=pltpu.CompilerParams(dimension_semantics=("arbitrary",), vmem_limit_bytes=VMEM_LIMIT),
        name="experts",
    )(blk_exp, blk_valid, xs, w_gu, b_gu, w_down, b_down)


def _combine_kernel(chunk_rows, n_total, x1_ref, route_ref, gate_ref, ys_hbm, out_ref, stage, sems):
    i = pl.program_id(0)
    n_steps = pl.num_programs(0)
    cur = i % 2

    def chunk_copy(slot, stage_row, sorted_row):
        return pltpu.make_async_copy(ys_hbm.at[pl.ds(_subrow(sorted_row), CHUNK_SUBROWS)],
                                     stage.at[slot, pl.ds(_subrow(stage_row), CHUNK_SUBROWS)], sems.at[slot])

    def fetch_tile(tile, slot):
        _tile_chunks(chunk_rows, n_total, tile, lambda c, sr, dr: chunk_copy(slot, sr, dr).start(priority=c % 2))

    def wait_tile(tile, slot):
        _tile_chunks(chunk_rows, n_total, tile, lambda c, sr, dr: chunk_copy(slot, 0, 0).wait())

    @pl.when(i == 0)
    def _():
        stage[...] = jnp.zeros(stage.shape, _f32)
        fetch_tile(0, 0)

    @pl.when(i + 1 < n_steps)
    def _():
        fetch_tile(i + 1, 1 - cur)

    wait_tile(i, cur)
    rows = _from_token_major(stage.at[cur], STAGE_ROWS).astype(_bf16)
    route = route_ref[...]
    gates = gate_ref[...]
    lane_i = lax.broadcasted_iota(jnp.int32, (ROUTE_TILE, STAGE_ROWS), 1)
    weight = jnp.zeros((ROUTE_TILE, STAGE_ROWS), _f32)
    for kk in range(TOP_K):
        weight = weight + jnp.where(lane_i == route[:, TOP_K + kk:TOP_K + kk + 1], gates[:, kk:kk + 1], 0.0)
    out_ref[...] = x1_ref[...] + _dot(weight.astype(_bf16), rows)


def _combine(x1, route, rgate, ys, sched):
    T = x1.shape[0]
    tc = ROUTE_TILE
    grid_spec = pltpu.PrefetchScalarGridSpec(
        num_scalar_prefetch=2,
        grid=(T // tc,),
        in_specs=[pl.BlockSpec((tc, D_MODEL), lambda i, *_: (i, 0)),
                  pl.BlockSpec((tc, LANES), lambda i, *_: (i, 0)),
                  pl.BlockSpec((tc, LANES), lambda i, *_: (i, 0)),
                  pl.BlockSpec(memory_space=pl.ANY)],
        out_specs=pl.BlockSpec((tc, D_MODEL), lambda i, *_: (i, 0)),
        scratch_shapes=[pltpu.VMEM((2, STAGE_ROWS * N_SLAB, LANES), _f32),
                        pltpu.SemaphoreType.DMA((2,))],
    )
    return pl.pallas_call(
        _combine_kernel,
        grid_spec=grid_spec,
        out_shape=jax.ShapeDtypeStruct((T, D_MODEL), _f32),
        compiler_params=pltpu.CompilerParams(dimension_semantics=("arbitrary",), vmem_limit_bytes=VMEM_LIMIT),
        name="combine",
    )(sched["read_rows"], sched["n_total"], x1, route, rgate, ys)


def _prepare_params(w):
    L = w["w_in"].shape[0]
    p = {}
    row = lambda a: a.reshape(L, 1, a.shape[-1])
    p["attn_norm_g"] = row(w["attn_norm_g"])
    p["w_in"] = jnp.pad(w["w_in"], ((0, 0), (0, 0), (0, D_IN_PAD - w["w_in"].shape[-1]))).astype(_bf16)
    eye = jnp.eye(len(POOL_WINDOWS), dtype=_f32)
    p["pool_w"] = jnp.einsum("lgcd,gh->lgchd", w["pool_w"], eye).reshape(L, POOL_W, POOL_W).astype(_bf16)
    p["pool_scale"] = row(w["pool_scale"])
    p["conv_dw"] = w["conv_dw"]
    p["conv_dw_b"] = row(w["conv_dw_b"])
    p["conv_ln_g"] = row(w["conv_ln_g"])
    p["conv_ln_b"] = row(w["conv_ln_b"])
    p["conv_pw"] = w["conv_pw"].astype(_bf16)
    p["conv_pw_b"] = row(w["conv_pw_b"])
    p["q_a_g"] = row(w["q_a_g"])
    wuq = w["w_uq"].reshape(L, Q_LORA, N_HEADS, QK_HEAD)
    p["w_uq"] = jnp.pad(wuq, ((0, 0), (0, 0), (0, 0), (0, HEAD_PAD - QK_HEAD))).reshape(
        L, Q_LORA, N_HEADS * HEAD_PAD).astype(_bf16)
    p["kv_a_g"] = row(w["kv_a_g"])
    wukv = w["w_ukv"].reshape(L, KV_LORA, N_HEADS, QK_NOPE + V_DIM)
    p["w_ukv"] = jnp.concatenate([wukv[..., :QK_NOPE].reshape(L, KV_LORA, N_HEADS * QK_NOPE),
                                  wukv[..., QK_NOPE:].reshape(L, KV_LORA, N_HEADS * V_DIM)], axis=-1).astype(_bf16)
    pad_head = lambda g: jnp.pad(g, ((0, 0), (0, HEAD_PAD - QK_HEAD))).reshape(L, 1, HEAD_PAD)
    p["q_norm_g"] = pad_head(w["q_norm_g"])
    p["k_norm_g"] = pad_head(w["k_norm_g"])
    p["pool_out_g"] = row(w["pool_out_g"])
    p["conv_out_g"] = row(w["conv_out_g"])
    p["attn_out_g"] = row(w["attn_out_g"])
    p["w_o"] = w["w_o"].astype(_bf16)
    p["ffn_norm_g"] = row(w["ffn_norm_g"])
    p["w_router"] = jnp.pad(w["w_router"], ((0, 0), (0, 0), (0, LANES - N_EXPERTS))).astype(_bf16)
    p["b_router"] = jnp.pad(w["b_router"], ((0, 0), (0, LANES - N_EXPERTS)),
                            constant_values=NEG_BIG).reshape(L, 1, LANES)
    E = w["w_gu"].shape[1]
    p["w_gu"] = w["w_gu"]
    bgu = w["b_gu"].reshape(L, E, 2 * D_FF // GU_GROUP, LANES, 2)
    p["b_gu"] = jnp.swapaxes(bgu, -1, -2).reshape(L, E, 1, 2 * D_FF)
    p["w_down"] = w["w_down"]
    p["b_down"] = w["b_down"].reshape(L, E, 1, D_MODEL)
    return p


def _rope_tables(positions):
    inv_freq = 1.0 / (ROPE_THETA ** (jnp.arange(0, QK_ROPE, 2, dtype=_f32) / QK_ROPE))
    ang = positions.astype(_f32)[..., None] * inv_freq
    cos, sin = jnp.cos(ang), jnp.sin(ang)
    zeros = jnp.zeros(cos.shape[:-1] + (LANES - QK_ROPE,), _f32)
    return (jnp.concatenate([cos, cos, zeros], axis=-1), jnp.concatenate([-sin, sin, zeros], axis=-1))


def _routing_schedule(cnt_f, n_tokens):
    n = cnt_f[:, 0:N_EXPERTS].astype(jnp.int32)
    counts = jnp.sum(n, axis=0)
    spare = ROW_CHUNK - 1
    padded = jnp.where(counts > 0, (counts + spare + EXPERT_BLOCK - 1) // EXPERT_BLOCK * EXPERT_BLOCK, 0)
    pend = jnp.cumsum(padded)
    pstart = pend - padded
    base = pstart[None, :] + jnp.cumsum(n, axis=0) - n
    n_chunks = (n + ROW_CHUNK - 1) // ROW_CHUNK
    max_rows = n_tokens * TOP_K + N_EXPERTS * (spare + EXPERT_BLOCK - 1) + ROW_CHUNK
    n_blocks = -(-max_rows // EXPERT_BLOCK)
    blk_row = jnp.arange(n_blocks, dtype=jnp.int32) * EXPERT_BLOCK
    blk_exp = jnp.minimum(jnp.sum((blk_row[:, None] >= pend[None, :]).astype(jnp.int32), axis=1), N_EXPERTS - 1)
    onehot = (blk_exp[:, None] == jnp.arange(N_EXPERTS, dtype=jnp.int32)[None, :]).astype(jnp.int32)
    seg_end = jnp.sum(onehot * (pstart + counts)[None, :], axis=1)
    blk_valid = jnp.where(blk_row < pend[-1], jnp.clip(seg_end - blk_row, 0, EXPERT_BLOCK), 0)
    i32 = lambda a: a.astype(jnp.int32)
    n_rows = n_blocks * EXPERT_BLOCK
    seg_chunks = (padded - counts + ROW_CHUNK - 1) // ROW_CHUNK
    pad_start = jnp.concatenate([pend - seg_chunks * ROW_CHUNK, pend[-1:]])
    pad_chunks = jnp.concatenate([seg_chunks, (n_rows + DUMP_ROWS - pend[-1:]) // ROW_CHUNK])
    cum_end = jnp.cumsum(n_chunks, axis=1)
    cum_start = cum_end - n_chunks
    c_idx = jnp.arange(MAX_TILE_CHUNKS, dtype=jnp.int32)
    e_of = jnp.minimum(jnp.sum((c_idx[None, :, None] >= cum_end[:, None, :]).astype(jnp.int32), axis=-1),
                       N_EXPERTS - 1)
    pick = (e_of[:, :, None] == jnp.arange(N_EXPERTS, dtype=jnp.int32)[None, None, :]).astype(jnp.int32)
    chunk_dst = (jnp.sum(pick * base[:, None, :], axis=-1)
                 + (c_idx[None, :] - jnp.sum(pick * cum_start[:, None, :], axis=-1)) * ROW_CHUNK)
    needed = c_idx[None, :] < cum_end[:, -1:]
    write_rows = jnp.where(needed, chunk_dst, n_rows + c_idx[None, :] * ROW_CHUNK)
    read_rows = jnp.where(needed, chunk_dst, 0)
    sched = dict(write_rows=i32(write_rows).reshape(-1), read_rows=i32(read_rows).reshape(-1),
                 n_total=i32(cum_end[:, -1]),
                 pad_start=i32(pad_start), pad_chunks=i32(pad_chunks))
    return sched, i32(blk_exp), i32(blk_valid), n_rows


def kernel(x, positions, attn_norm_g, w_in, pool_w, pool_scale, conv_dw, conv_dw_b, conv_ln_g, conv_ln_b, conv_pw, conv_pw_b, q_a_g, w_uq, kv_a_g, w_ukv, q_norm_g, k_norm_g, pool_out_g, conv_out_g, attn_out_g, w_o, ffn_norm_g, w_router, b_router, w_gu, b_gu, w_down, b_down):
    B, S, D = x.shape
    T = B * S
    depth = w_in.shape[0]
    p = _prepare_params(dict(
        attn_norm_g=attn_norm_g, w_in=w_in, pool_w=pool_w, pool_scale=pool_scale, conv_dw=conv_dw,
        conv_dw_b=conv_dw_b, conv_ln_g=conv_ln_g, conv_ln_b=conv_ln_b, conv_pw=conv_pw, conv_pw_b=conv_pw_b,
        q_a_g=q_a_g, w_uq=w_uq, kv_a_g=kv_a_g, w_ukv=w_ukv, q_norm_g=q_norm_g, k_norm_g=k_norm_g,
        pool_out_g=pool_out_g, conv_out_g=conv_out_g, attn_out_g=attn_out_g, w_o=w_o, ffn_norm_g=ffn_norm_g,
        w_router=w_router, b_router=b_router, w_gu=w_gu, b_gu=b_gu, w_down=w_down, b_down=b_down))
    rope_c, rope_s = _rope_tables(positions)
    for layer in range(depth):
        mix, q, k, v = _mixer_front(layer, x, rope_c, rope_s, p)
        att = _attention(q, k, v)
        x1, h2, route, rgate, cnt = _post_mix(layer, x.reshape(T, D), mix.reshape(T, -1), att.reshape(T, -1), p)
        sched, blk_exp, blk_valid, n_rows = _routing_schedule(cnt, T)
        slots_t = route[:, TOP_K:2 * TOP_K].reshape(T // ROUTE_TILE, ROUTE_TILE, TOP_K).transpose(0, 2, 1)
        slots_t = jnp.pad(slots_t, ((0, 0), (0, SUBLANES - TOP_K), (0, 0)), constant_values=-1)
        xs = _dispatch(h2, slots_t, sched, n_rows)
        ys = _experts(layer, xs, n_rows, blk_exp, blk_valid, p["w_gu"], p["b_gu"], p["w_down"], p["b_down"])
        x = _combine(x1, route, rgate, ys, sched).reshape(B, S, D)
    return x
```

```python
import functools
import math

import jax
import jax.numpy as jnp
from jax import lax
from jax.experimental import pallas as pl
from jax.experimental.pallas import tpu as pltpu

D_MODEL = 1024
CHUNK = 64
POOL_W = 256
POOL_WINDOWS = (2, 4, 8, 16)
POOL_GC = 64
CONV_W = 256
CONV_K = 31
N_HEADS = 4
QK_NOPE = 128
QK_ROPE = 64
QK_HEAD = QK_NOPE + QK_ROPE
V_DIM = 128
Q_LORA = 256
KV_LORA = 128
ATTN_W = N_HEADS * V_DIM
ROPE_THETA = 10000.0
N_EXPERTS = 32
TOP_K = 4
D_FF = D_MODEL
SWIGLU_LIMIT = 7.0
SWIGLU_ALPHA = 1.702
EPS = 1e-6

LANES = 128
HEAD_PAD = 2 * LANES
D_IN_PAD = 1280
POOL_HALO = 16
CONV_HALO = 32
NEG_BIG = -1e30

TM_FRONT = 512
TQ = 512
TM_POST = 1024
ROUTE_TILE = 128
ROW_CHUNK = 8
SUBLANES = 8
STAGE_ROWS = -(-(ROUTE_TILE * TOP_K + N_EXPERTS * (ROW_CHUNK - 1)) // LANES) * LANES
EXPERT_BLOCK = 512
FF_CHUNK = 256
VMEM_LIMIT = 56 * 1024 * 1024

_bf16 = jnp.bfloat16
_f32 = jnp.float32


def _dot(a, b):
    return jnp.dot(a, b, preferred_element_type=_f32)


def _rms(x, width):
    return lax.rsqrt(jnp.sum(x * x, axis=-1, keepdims=True) * (1.0 / width) + EPS)


def _layer_spec(shape, layer, n_grid):
    nd = len(shape)
    block = (None,) + tuple(shape[1:])
    if n_grid == 1:
        return pl.BlockSpec(block, lambda i: (layer,) + (0,) * (nd - 1))
    return pl.BlockSpec(block, lambda i, j: (layer,) + (0,) * (nd - 1))


def _rope(x, c_tab, s_tab):
    lane = lax.broadcasted_iota(jnp.int32, x.shape, 1)
    partner = jnp.where(lane < QK_ROPE // 2, pltpu.roll(x, LANES - QK_ROPE // 2, 1),
                        pltpu.roll(x, QK_ROPE // 2, 1))
    return x * c_tab + partner * s_tab


def _front_kernel(x_ref, ropec_ref, ropes_ref, ang_ref, win_ref, poolw_ref, pscale_ref, cdw_ref, cdwb_ref,
                  clng_ref, clnb_ref, cpw_ref, cpwb_ref, qag_ref, wuq_ref, kvag_ref, wukv_ref, qng_ref,
                  kng_ref, poutg_ref, coutg_ref,
                  mix_ref, q_ref, k_ref, v_ref, pool_ext, conv_ext):
    j = pl.program_id(1)
    tm = x_ref.shape[0]

    x = x_ref[...]
    h = x * _rms(x, D_MODEL) * ang_ref[...]
    z = _dot(h.astype(_bf16), win_ref[...])

    u = z[:, 0:POOL_W]

    @pl.when(j == 0)
    def _():
        pool_ext[0:POOL_HALO, :] = jnp.zeros((POOL_HALO, POOL_W), _f32)
        conv_ext[0:CONV_HALO, :] = jnp.zeros((CONV_HALO, CONV_W), _f32)

    @pl.when(j > 0)
    def _():
        pool_ext[0:POOL_HALO, :] = pool_ext[tm:tm + POOL_HALO, :]
        conv_ext[0:CONV_HALO, :] = conv_ext[tm:tm + CONV_HALO, :]

    pool_ext[POOL_HALO:POOL_HALO + tm, :] = u
    t_pos = (j * tm + lax.broadcasted_iota(jnp.int32, (tm, LANES), 0) + 1).astype(_f32)
    lane = lax.broadcasted_iota(jnp.int32, (tm, LANES), 1)
    pooled_halves = []
    for half, (w_lo, w_hi) in enumerate(((POOL_WINDOWS[0], POOL_WINDOWS[1]),
                                         (POOL_WINDOWS[2], POOL_WINDOWS[3]))):
        cols = slice(half * LANES, (half + 1) * LANES)
        s_lo = pool_ext[POOL_HALO:POOL_HALO + tm, cols]
        for d in range(1, w_lo):
            s_lo = s_lo + pool_ext[POOL_HALO - d:POOL_HALO - d + tm, cols]
        s_hi = s_lo
        for d in range(w_lo, w_hi):
            s_hi = s_hi + pool_ext[POOL_HALO - d:POOL_HALO - d + tm, cols]
        first = lane < POOL_GC
        win_sum = jnp.where(first, s_lo, s_hi)
        cnt = jnp.minimum(t_pos, jnp.where(first, float(w_lo), float(w_hi)))
        pooled_halves.append(win_sum / cnt - u[:, cols])
    pooled = jnp.concatenate(pooled_halves, axis=-1)
    y_pool = _dot(pooled.astype(_bf16), poolw_ref[...]) * pscale_ref[...]
    y_pool = y_pool * _rms(y_pool, POOL_W) * poutg_ref[...]

    a = z[:, POOL_W:POOL_W + CONV_W]
    gate = z[:, POOL_W + CONV_W:POOL_W + 2 * CONV_W]
    conv_ext[CONV_HALO:CONV_HALO + tm, :] = a * jax.nn.sigmoid(gate)
    base = CONV_HALO - (CONV_K - 1)
    acc = cdwb_ref[...]
    window = conv_ext[...]
    n_win = window.shape[0]
    for r in range(SUBLANES):
        tiles = [(base + kk) // SUBLANES for kk in range(CONV_K) if (base + kk) % SUBLANES == r]
        shifted = window if r == 0 else pltpu.roll(window, n_win - r, 0)
        for m in tiles:
            kk = m * SUBLANES + r - base
            acc = acc + shifted[m * SUBLANES:m * SUBLANES + tm, :] * cdw_ref[kk:kk + 1, :]
    mu = jnp.mean(acc, axis=-1, keepdims=True)
    cen = acc - mu
    var = jnp.mean(cen * cen, axis=-1, keepdims=True)
    ln = cen * lax.rsqrt(var + EPS) * clng_ref[...] + clnb_ref[...]
    sw = ln * jax.nn.sigmoid(ln)
    y_conv = _dot(sw.astype(_bf16), cpw_ref[...]) + cpwb_ref[...]
    y_conv = y_conv * _rms(y_conv, CONV_W) * coutg_ref[...]

    mix_ref[:, 0:POOL_W] = y_pool.astype(_bf16)
    mix_ref[:, POOL_W:POOL_W + CONV_W] = y_conv.astype(_bf16)

    s1 = POOL_W + 2 * CONV_W
    c_q = z[:, s1:s1 + Q_LORA]
    c_kv = z[:, s1 + Q_LORA:s1 + Q_LORA + KV_LORA]
    k_rope = z[:, s1 + Q_LORA + KV_LORA:s1 + Q_LORA + KV_LORA + LANES]
    c_tab = ropec_ref[...]
    s_tab = ropes_ref[...]

    q_all = _dot((c_q * _rms(c_q, Q_LORA) * qag_ref[...]).astype(_bf16), wuq_ref[...])
    kv_all = _dot((c_kv * _rms(c_kv, KV_LORA) * kvag_ref[...]).astype(_bf16), wukv_ref[...])
    sm_scale = math.log2(math.e) / math.sqrt(QK_HEAD)
    kr_ss = jnp.sum(k_rope * k_rope, axis=-1, keepdims=True)
    kr_rot = _rope(k_rope * kng_ref[:, LANES:2 * LANES], c_tab, s_tab)
    for hd in range(N_HEADS):
        qh = q_all[:, hd * HEAD_PAD:(hd + 1) * HEAD_PAD]
        qh = qh * (_rms(qh, QK_HEAD) * sm_scale) * qng_ref[...]
        q_ref[:, hd * HEAD_PAD:hd * HEAD_PAD + LANES] = qh[:, 0:LANES].astype(_bf16)
        q_ref[:, hd * HEAD_PAD + LANES:(hd + 1) * HEAD_PAD] = _rope(qh[:, LANES:], c_tab, s_tab).astype(_bf16)
        kn = kv_all[:, hd * QK_NOPE:(hd + 1) * QK_NOPE]
        k_rs = lax.rsqrt((jnp.sum(kn * kn, axis=-1, keepdims=True) + kr_ss) * (1.0 / QK_HEAD) + EPS)
        k_ref[:, hd * HEAD_PAD:hd * HEAD_PAD + LANES] = (kn * k_rs * kng_ref[:, 0:LANES]).astype(_bf16)
        k_ref[:, hd * HEAD_PAD + LANES:(hd + 1) * HEAD_PAD] = (kr_rot * k_rs).astype(_bf16)
        v_lo = hd * (V_DIM + LANES)
        v_ref[:, v_lo:v_lo + V_DIM] = kv_all[:, (N_HEADS + hd) * V_DIM:(N_HEADS + hd + 1) * V_DIM].astype(_bf16)
        v_ref[:, v_lo + V_DIM:v_lo + V_DIM + LANES] = jnp.ones((tm, LANES), _bf16)


def _mixer_front(layer, x, rope_c, rope_s, p):
    B, S, _ = x.shape
    tm = TM_FRONT
    row = lambda w: pl.BlockSpec((None, tm, w), lambda b, j: (b, j, 0))
    names = ("attn_norm_g", "w_in", "pool_w", "pool_scale", "conv_dw", "conv_dw_b", "conv_ln_g", "conv_ln_b",
             "conv_pw", "conv_pw_b", "q_a_g", "w_uq", "kv_a_g", "w_ukv", "q_norm_g", "k_norm_g",
             "pool_out_g", "conv_out_g")
    weights = [p[n] for n in names]
    in_specs = [row(D_MODEL), row(LANES), row(LANES)] + [_layer_spec(w.shape, layer, 2) for w in weights]
    out_shape = (jax.ShapeDtypeStruct((B, S, POOL_W + CONV_W), _bf16),
                 jax.ShapeDtypeStruct((B, S, N_HEADS * HEAD_PAD), _bf16),
                 jax.ShapeDtypeStruct((B, S, N_HEADS * HEAD_PAD), _bf16),
                 jax.ShapeDtypeStruct((B, S, N_HEADS * (V_DIM + LANES)), _bf16))
    out_specs = (row(POOL_W + CONV_W), row(N_HEADS * HEAD_PAD), row(N_HEADS * HEAD_PAD),
                 row(N_HEADS * (V_DIM + LANES)))
    return pl.pallas_call(
        _front_kernel,
        grid=(B, S // tm),
        in_specs=in_specs,
        out_specs=out_specs,
        out_shape=out_shape,
        scratch_shapes=[pltpu.VMEM((tm + POOL_HALO, POOL_W), _f32),
                        pltpu.VMEM((tm + CONV_HALO, CONV_W), _f32)],
        compiler_params=pltpu.CompilerParams(dimension_semantics=("arbitrary", "arbitrary"),
                                             vmem_limit_bytes=VMEM_LIMIT),
        name="mixer_front",
    )(x, rope_c, rope_s, *weights)


def _attn_kernel(q_ref, qnext_ref, k_ref, v_ref, o_ref, s_scr, m_scr):
    qi = pl.program_id(2)
    tq = q_ref.shape[0]
    n_tiles = k_ref.shape[0] // tq
    q_chunk = lax.broadcasted_iota(jnp.int32, (tq, tq), 0) // CHUNK
    k_chunk = lax.broadcasted_iota(jnp.int32, (tq, tq), 1) // CHUNK

    def score_phase(n, q_tile_ref):
        keys = (n + 1) * tq
        s = lax.dot_general(q_tile_ref[...], k_ref[0:keys, :], (((1,), (1,)), ((), ())),
                            preferred_element_type=_f32)
        diag = jnp.where(k_chunk <= q_chunk, s[:, n * tq:], NEG_BIG)
        s = diag if n == 0 else jnp.concatenate([s[:, :n * tq], diag], axis=-1)
        m_scr[n % 2] = jnp.broadcast_to(jnp.max(s, axis=-1, keepdims=True), (tq, LANES))
        s_scr[n % 2, :, 0:keys] = s

    def value_phase(n):
        keys = (n + 1) * tq
        p = jnp.exp2(s_scr[n % 2, :, 0:keys] - m_scr[n % 2, :, 0:1]).astype(_bf16)
        acc = _dot(p, v_ref[0:keys, :])
        o_ref[...] = (acc[:, 0:V_DIM] / acc[:, V_DIM:V_DIM + 1]).astype(o_ref.dtype)

    def step(n):
        if n == 0:
            score_phase(0, q_ref)
        value_phase(n)
        if n + 1 < n_tiles:
            score_phase(n + 1, qnext_ref)

    for n in range(n_tiles):
        pl.when(qi == n)(functools.partial(step, n))


def _attention(q, k, v):
    B, S, _ = q.shape
    nq = S // TQ
    return pl.pallas_call(
        _attn_kernel,
        grid=(B, N_HEADS, nq),
        in_specs=[pl.BlockSpec((None, TQ, HEAD_PAD), lambda b, h, i: (b, i, h)),
                  pl.BlockSpec((None, TQ, HEAD_PAD), lambda b, h, i: (b, jnp.minimum(i + 1, nq - 1), h)),
                  pl.BlockSpec((None, S, HEAD_PAD), lambda b, h, i: (b, 0, h)),
                  pl.BlockSpec((None, S, V_DIM + LANES), lambda b, h, i: (b, 0, h))],
        out_specs=pl.BlockSpec((None, TQ, V_DIM), lambda b, h, i: (b, i, h)),
        out_shape=jax.ShapeDtypeStruct((B, S, ATTN_W), _bf16),
        scratch_shapes=[pltpu.VMEM((2, TQ, S), _f32), pltpu.VMEM((2, TQ, LANES), _f32)],
        compiler_params=pltpu.CompilerParams(dimension_semantics=("arbitrary", "arbitrary", "arbitrary"),
                                             vmem_limit_bytes=VMEM_LIMIT),
        name="attention",
    )(q, q, k, v)


def _post_kernel(x_ref, mix_ref, att_ref, aog_ref, wo_ref, fng_ref, wr_ref, br_ref,
                 x1_ref, h2_ref, route_ref, rgate_ref, cnt_ref):
    tm = x_ref.shape[0]
    att = att_ref[...].astype(_f32)
    att_n = (att * _rms(att, ATTN_W) * aog_ref[...]).astype(_bf16)
    x1 = x_ref[...] + _dot(mix_ref[...], wo_ref[0:POOL_W + CONV_W, :]) + _dot(att_n, wo_ref[POOL_W + CONV_W:, :])
    x1_ref[...] = x1
    h2 = (x1 * _rms(x1, D_MODEL) * fng_ref[...]).astype(_bf16)
    h2_ref[...] = h2

    logits = _dot(h2, wr_ref[...]) + br_ref[...]
    lane = lax.broadcasted_iota(jnp.int32, logits.shape, 1)
    lane_f = lane.astype(_f32)
    work = logits
    sel = jnp.zeros(logits.shape, _f32)
    onehots, vals, idxs = [], [], []
    for _ in range(TOP_K):
        mx = jnp.max(work, axis=-1, keepdims=True)
        idx_f = jnp.min(jnp.where(work == mx, lane_f, float(LANES)), axis=-1, keepdims=True)
        idx = idx_f.astype(jnp.int32)
        hot = lane == idx
        onehots.append(hot)
        vals.append(mx)
        idxs.append(idx)
        sel = jnp.where(hot, 1.0, sel)
        work = jnp.where(hot, -jnp.inf, work)
    exps = [jnp.exp(vk - vals[0]) for vk in vals]
    denom = exps[0] + exps[1] + exps[2] + exps[3]

    nt = tm // ROUTE_TILE
    r_i = lax.broadcasted_iota(jnp.int32, (ROUTE_TILE, ROUTE_TILE), 0)
    c_i = lax.broadcasted_iota(jnp.int32, (ROUTE_TILE, ROUTE_TILE), 1)
    tri = jnp.where(c_i < r_i, 1.0, 0.0).astype(_bf16)
    upper = jnp.where(r_i < c_i, 1.0, 0.0).astype(_bf16)
    sel_b = sel.astype(_bf16)
    ranks, cnts = [], []
    for s in range(nt):
        rows = slice(s * ROUTE_TILE, (s + 1) * ROUTE_TILE)
        ranks.append(_dot(tri, sel_b[rows, :]))
        cnts.append(jnp.sum(sel[rows, :], axis=0, keepdims=True))
    cnt = jnp.concatenate(cnts, axis=0)
    cnt_ref[...] = cnt
    chunks = jnp.floor((cnt + (ROW_CHUNK - 1)) * (1.0 / ROW_CHUNK))
    boff = _dot(chunks.astype(_bf16), upper) * float(ROW_CHUNK)
    slot = jnp.concatenate(
        [ranks[s] + jnp.broadcast_to(boff[s:s + 1, :], (ROUTE_TILE, LANES)) for s in range(nt)], axis=0)

    route = jnp.zeros(logits.shape, jnp.int32)
    rgate = jnp.zeros(logits.shape, _f32)
    for kk in range(TOP_K):
        col = jnp.sum(jnp.where(onehots[kk], slot, 0.0), axis=-1, keepdims=True).astype(jnp.int32)
        route = jnp.where(lane == kk, idxs[kk], route)
        route = jnp.where(lane == TOP_K + kk, col, route)
        rgate = jnp.where(lane == kk, exps[kk] / denom, rgate)
    route_ref[...] = route
    rgate_ref[...] = rgate


def _post_mix(layer, x, mix, att, p):
    T = x.shape[0]
    tm = TM_POST
    row = lambda w: pl.BlockSpec((tm, w), lambda i: (i, 0))
    names = ("attn_out_g", "w_o", "ffn_norm_g", "w_router", "b_router")
    weights = [p[n] for n in names]
    out_shape = (jax.ShapeDtypeStruct((T, D_MODEL), _f32), jax.ShapeDtypeStruct((T, D_MODEL), _bf16),
                 jax.ShapeDtypeStruct((T, LANES), jnp.int32), jax.ShapeDtypeStruct((T, LANES), _f32),
                 jax.ShapeDtypeStruct((T // ROUTE_TILE, LANES), _f32))
    return pl.pallas_call(
        _post_kernel,
        grid=(T // tm,),
        in_specs=[row(D_MODEL), row(POOL_W + CONV_W), row(ATTN_W)] + [_layer_spec(w.shape, layer, 1) for w in weights],
        out_specs=(row(D_MODEL), row(D_MODEL), row(LANES), row(LANES),
                   pl.BlockSpec((tm // ROUTE_TILE, LANES), lambda i: (i, 0))),
        out_shape=out_shape,
        compiler_params=pltpu.CompilerParams(dimension_semantics=("arbitrary",), vmem_limit_bytes=VMEM_LIMIT),
        name="post_mix",
    )(x, mix, att, *weights)


N_SLAB = D_MODEL // LANES
assert N_SLAB == SUBLANES
CHUNK_SUBROWS = ROW_CHUNK * N_SLAB


def _to_token_major(ref, value):
    rows = value.shape[0]
    for j in range(N_SLAB):
        ref[pl.ds(j, rows, stride=N_SLAB), :] = value[:, j * LANES:(j + 1) * LANES]


def _from_token_major(ref, rows):
    return jnp.concatenate([ref[pl.ds(j, rows, stride=N_SLAB), :] for j in range(N_SLAB)], axis=-1)


def _subrow(row):
    return pl.multiple_of(row * N_SLAB, N_SLAB)


MAX_TILE_CHUNKS = ROUTE_TILE * TOP_K // ROW_CHUNK + N_EXPERTS
assert MAX_TILE_CHUNKS * ROW_CHUNK == STAGE_ROWS
DUMP_ROWS = STAGE_ROWS


ALWAYS_CHUNKS = ROUTE_TILE * TOP_K // ROW_CHUNK + N_EXPERTS // 4
CHUNK_GROUP = 8
assert (MAX_TILE_CHUNKS - ALWAYS_CHUNKS) % CHUNK_GROUP == 0


def _tile_chunks(chunk_rows, n_total, tile, chunk_fn):
    def run(lo, hi):
        for c in range(lo, hi):
            chunk_fn(c, c * ROW_CHUNK, chunk_rows[tile * MAX_TILE_CHUNKS + c])

    run(0, ALWAYS_CHUNKS)
    for lo in range(ALWAYS_CHUNKS, MAX_TILE_CHUNKS, CHUNK_GROUP):
        pl.when(n_total[tile] > lo)(functools.partial(run, lo, lo + CHUNK_GROUP))


def _dispatch_kernel(chunk_rows, n_total, pad_start, pad_chunks,
                     slot_ref, h2_ref, xs_hbm, stage, zeros, sems, zsem):
    i = pl.program_id(0)
    n_steps = pl.num_programs(0)
    cur = i % 2

    def chunk_copy(slot, stage_row, sorted_row):
        return pltpu.make_async_copy(stage.at[slot, pl.ds(_subrow(stage_row), CHUNK_SUBROWS)],
                                     xs_hbm.at[pl.ds(_subrow(sorted_row), CHUNK_SUBROWS)], sems.at[slot])

    def send_tile(tile, slot):
        _tile_chunks(chunk_rows, n_total, tile, lambda c, sr, dr: chunk_copy(slot, sr, dr).start(priority=c % 2))

    def wait_tile(tile, slot):
        _tile_chunks(chunk_rows, n_total, tile, lambda c, sr, dr: chunk_copy(slot, 0, 0).wait())

    @pl.when(i == 0)
    def _():
        zeros[...] = jnp.zeros(zeros.shape, _f32)

        def zero_copy(row):
            return pltpu.make_async_copy(zeros, xs_hbm.at[pl.ds(_subrow(row), CHUNK_SUBROWS)], zsem)

        for e in range(N_EXPERTS + 1):
            def start(c, carry, e=e):
                zero_copy(pad_start[e] + c * ROW_CHUNK).start()
                return carry
            lax.fori_loop(0, pad_chunks[e], start, 0)
        for e in range(N_EXPERTS + 1):
            def wait(c, carry, e=e):
                zero_copy(pad_start[e] + c * ROW_CHUNK).wait()
                return carry
            lax.fori_loop(0, pad_chunks[e], wait, 0)

    slots = slot_ref[...]
    row_i = lax.broadcasted_iota(jnp.int32, (STAGE_ROWS, ROUTE_TILE), 0)
    place = jnp.zeros((STAGE_ROWS, ROUTE_TILE), _f32)
    for kk in range(TOP_K):
        place = place + jnp.where(row_i == slots[kk:kk + 1, :], 1.0, 0.0)
    staged = _dot(place.astype(_bf16), h2_ref[...])
    _to_token_major(stage.at[cur], staged)

    @pl.when(i > 0)
    def _():
        wait_tile(i - 1, 1 - cur)

    send_tile(i, cur)

    @pl.when(i == n_steps - 1)
    def _():
        wait_tile(i, cur)


def _dispatch(h2, slots_t, sched, n_rows):
    T = h2.shape[0]
    grid_spec = pltpu.PrefetchScalarGridSpec(
        num_scalar_prefetch=4,
        grid=(T // ROUTE_TILE,),
        in_specs=[pl.BlockSpec((None, SUBLANES, ROUTE_TILE), lambda i, *_: (i, 0, 0)),
                  pl.BlockSpec((ROUTE_TILE, D_MODEL), lambda i, *_: (i, 0))],
        out_specs=pl.BlockSpec(memory_space=pl.ANY),
        scratch_shapes=[pltpu.VMEM((2, STAGE_ROWS * N_SLAB, LANES), _f32),
                        pltpu.VMEM((CHUNK_SUBROWS, LANES), _f32),
                        pltpu.SemaphoreType.DMA((2,)), pltpu.SemaphoreType.DMA(())],
    )
    return pl.pallas_call(
        _dispatch_kernel,
        grid_spec=grid_spec,
        out_shape=jax.ShapeDtypeStruct(((n_rows + DUMP_ROWS) * N_SLAB, LANES), _f32),
        compiler_params=pltpu.CompilerParams(dimension_semantics=("arbitrary",), vmem_limit_bytes=VMEM_LIMIT),
        name="dispatch",
    )(sched["write_rows"], sched["n_total"], sched["pad_start"], sched["pad_chunks"], slots_t, h2)


GU_GROUP = 2 * LANES


def _expert_kernel(blk_exp, blk_valid, xs_ref, wgu_ref, bgu_ref, wd_ref, bd_ref, y_ref, wgu_s, wd_s):
    i = pl.program_id(0)
    valid = blk_valid[i]

    @pl.when(jnp.logical_or(i == 0, blk_exp[i] != blk_exp[jnp.maximum(i - 1, 0)]))
    def _():
        r_i = lax.broadcasted_iota(jnp.int32, (GU_GROUP, GU_GROUP), 0)
        c_i = lax.broadcasted_iota(jnp.int32, (GU_GROUP, GU_GROUP), 1)
        src = jnp.where(c_i < LANES, 2 * c_i, 2 * (c_i - LANES) + 1)
        perm = jnp.where(r_i == src, 1.0, 0.0).astype(_bf16)
        for b in range(2 * D_FF // GU_GROUP):
            cols = slice(b * GU_GROUP, (b + 1) * GU_GROUP)
            wgu_s[:, cols] = _dot(wgu_ref[:, cols].astype(_bf16), perm).astype(_bf16)
        wd_s[...] = wd_ref[...].astype(_bf16)

    @pl.when(valid > 0)
    def _():
        x = _from_token_major(xs_ref, EXPERT_BLOCK).astype(_bf16)
        y = jnp.zeros((EXPERT_BLOCK, D_MODEL), _f32)
        groups = FF_CHUNK // LANES
        for c in range(D_FF // FF_CHUNK):
            lo = 2 * c * FF_CHUNK
            h = _dot(x, wgu_s[:, lo:lo + 2 * FF_CHUNK]) + bgu_ref[:, lo:lo + 2 * FF_CHUNK]
            hg = jnp.concatenate([h[:, (2 * q) * LANES:(2 * q + 1) * LANES] for q in range(groups)], axis=-1)
            hl = jnp.concatenate([h[:, (2 * q + 1) * LANES:(2 * q + 2) * LANES] for q in range(groups)], axis=-1)
            g = jnp.minimum(hg, SWIGLU_LIMIT)
            lin = jnp.clip(hl, -SWIGLU_LIMIT, SWIGLU_LIMIT)
            act = g * jax.nn.sigmoid(SWIGLU_ALPHA * g) * (lin + 1.0)
            y = y + _dot(act.astype(_bf16), wd_s[c * FF_CHUNK:(c + 1) * FF_CHUNK, :])
        _to_token_major(y_ref, y + bd_ref[...])

    @pl.when(valid == 0)
    def _():
        y_ref[...] = jnp.zeros(y_ref.shape, _f32)


def _experts(layer, xs, n_rows, blk_exp, blk_valid, w_gu, b_gu, w_down, b_down):
    nb = n_rows // EXPERT_BLOCK
    grid_spec = pltpu.PrefetchScalarGridSpec(
        num_scalar_prefetch=2,
        grid=(nb,),
        in_specs=[pl.BlockSpec((EXPERT_BLOCK * N_SLAB, LANES), lambda i, be, bv: (i, 0)),
                  pl.BlockSpec((None, None, D_MODEL, 2 * D_FF), lambda i, be, bv: (layer, be[i], 0, 0)),
                  pl.BlockSpec((None, None, 1, 2 * D_FF), lambda i, be, bv: (layer, be[i], 0, 0)),
                  pl.BlockSpec((None, None, D_FF, D_MODEL), lambda i, be, bv: (layer, be[i], 0, 0)),
                  pl.BlockSpec((None, None, 1, D_MODEL), lambda i, be, bv: (layer, be[i], 0, 0))],
        out_specs=pl.BlockSpec((EXPERT_BLOCK * N_SLAB, LANES), lambda i, be, bv: (i, 0)),
        scratch_shapes=[pltpu.VMEM((D_MODEL, 2 * D_FF), _bf16), pltpu.VMEM((D_FF, D_MODEL), _bf16)],
    )
    return pl.pallas_call(
        _expert_kernel,
        grid_spec=grid_spec,
        out_shape=jax.ShapeDtypeStruct((n_rows * N_SLAB, LANES), _f32),
        compiler_params=pltpu.CompilerParams(dimension_semantics=("arbitrary",), vmem_limit_bytes=VMEM_LIMIT),
        name="experts",
    )(blk_exp, blk_valid, xs, w_gu, b_gu, w_down, b_down)


def _combine_kernel(chunk_rows, n_total, x1_ref, route_ref, gate_ref, ys_hbm, out_ref, stage, sems):
    i = pl.program_id(0)
    n_steps = pl.num_programs(0)
    cur = i % 2

    def chunk_copy(slot, stage_row, sorted_row):
        return pltpu.make_async_copy(ys_hbm.at[pl.ds(_subrow(sorted_row), CHUNK_SUBROWS)],
                                     stage.at[slot, pl.ds(_subrow(stage_row), CHUNK_SUBROWS)], sems.at[slot])

    def fetch_tile(tile, slot):
        _tile_chunks(chunk_rows, n_total, tile, lambda c, sr, dr: chunk_copy(slot, sr, dr).start(priority=c % 2))

    def wait_tile(tile, slot):
        _tile_chunks(chunk_rows, n_total, tile, lambda c, sr, dr: chunk_copy(slot, 0, 0).wait())

    @pl.when(i == 0)
    def _():
        stage[...] = jnp.zeros(stage.shape, _f32)
        fetch_tile(0, 0)

    @pl.when(i + 1 < n_steps)
    def _():
        fetch_tile(i + 1, 1 - cur)

    wait_tile(i, cur)
    rows = _from_token_major(stage.at[cur], STAGE_ROWS).astype(_bf16)
    route = route_ref[...]
    gates = gate_ref[...]
    lane_i = lax.broadcasted_iota(jnp.int32, (ROUTE_TILE, STAGE_ROWS), 1)
    weight = jnp.zeros((ROUTE_TILE, STAGE_ROWS), _f32)
    for kk in range(TOP_K):
        weight = weight + jnp.where(lane_i == route[:, TOP_K + kk:TOP_K + kk + 1], gates[:, kk:kk + 1], 0.0)
    out_ref[...] = x1_ref[...] + _dot(weight.astype(_bf16), rows)


def _combine(x1, route, rgate, ys, sched):
    T = x1.shape[0]
    tc = ROUTE_TILE
    grid_spec = pltpu.PrefetchScalarGridSpec(
        num_scalar_prefetch=2,
        grid=(T // tc,),
        in_specs=[pl.BlockSpec((tc, D_MODEL), lambda i, *_: (i, 0)),
                  pl.BlockSpec((tc, LANES), lambda i, *_: (i, 0)),
                  pl.BlockSpec((tc, LANES), lambda i, *_: (i, 0)),
                  pl.BlockSpec(memory_space=pl.ANY)],
        out_specs=pl.BlockSpec((tc, D_MODEL), lambda i, *_: (i, 0)),
        scratch_shapes=[pltpu.VMEM((2, STAGE_ROWS * N_SLAB, LANES), _f32),
                        pltpu.SemaphoreType.DMA((2,))],
    )
    return pl.pallas_call(
        _combine_kernel,
        grid_spec=grid_spec,
        out_shape=jax.ShapeDtypeStruct((T, D_MODEL), _f32),
        compiler_params=pltpu.CompilerParams(dimension_semantics=("arbitrary",), vmem_limit_bytes=VMEM_LIMIT),
        name="combine",
    )(sched["read_rows"], sched["n_total"], x1, route, rgate, ys)


def _prepare_params(w):
    L = w["w_in"].shape[0]
    p = {}
    row = lambda a: a.reshape(L, 1, a.shape[-1])
    p["attn_norm_g"] = row(w["attn_norm_g"])
    p["w_in"] = jnp.pad(w["w_in"], ((0, 0), (0, 0), (0, D_IN_PAD - w["w_in"].shape[-1]))).astype(_bf16)
    eye = jnp.eye(len(POOL_WINDOWS), dtype=_f32)
    p["pool_w"] = jnp.einsum("lgcd,gh->lgchd", w["pool_w"], eye).reshape(L, POOL_W, POOL_W).astype(_bf16)
    p["pool_scale"] = row(w["pool_scale"])
    p["conv_dw"] = w["conv_dw"]
    p["conv_dw_b"] = row(w["conv_dw_b"])
    p["conv_ln_g"] = row(w["conv_ln_g"])
    p["conv_ln_b"] = row(w["conv_ln_b"])
    p["conv_pw"] = w["conv_pw"].astype(_bf16)
    p["conv_pw_b"] = row(w["conv_pw_b"])
    p["q_a_g"] = row(w["q_a_g"])
    wuq = w["w_uq"].reshape(L, Q_LORA, N_HEADS, QK_HEAD)
    p["w_uq"] = jnp.pad(wuq, ((0, 0), (0, 0), (0, 0), (0, HEAD_PAD - QK_HEAD))).reshape(
        L, Q_LORA, N_HEADS * HEAD_PAD).astype(_bf16)
    p["kv_a_g"] = row(w["kv_a_g"])
    wukv = w["w_ukv"].reshape(L, KV_LORA, N_HEADS, QK_NOPE + V_DIM)
    p["w_ukv"] = jnp.concatenate([wukv[..., :QK_NOPE].reshape(L, KV_LORA, N_HEADS * QK_NOPE),
                                  wukv[..., QK_NOPE:].reshape(L, KV_LORA, N_HEADS * V_DIM)], axis=-1).astype(_bf16)
    pad_head = lambda g: jnp.pad(g, ((0, 0), (0, HEAD_PAD - QK_HEAD))).reshape(L, 1, HEAD_PAD)
    p["q_norm_g"] = pad_head(w["q_norm_g"])
    p["k_norm_g"] = pad_head(w["k_norm_g"])
    p["pool_out_g"] = row(w["pool_out_g"])
    p["conv_out_g"] = row(w["conv_out_g"])
    p["attn_out_g"] = row(w["attn_out_g"])
    p["w_o"] = w["w_o"].astype(_bf16)
    p["ffn_norm_g"] = row(w["ffn_norm_g"])
    p["w_router"] = jnp.pad(w["w_router"], ((0, 0), (0, 0), (0, LANES - N_EXPERTS))).astype(_bf16)
    p["b_router"] = jnp.pad(w["b_router"], ((0, 0), (0, LANES - N_EXPERTS)),
                            constant_values=NEG_BIG).reshape(L, 1, LANES)
    E = w["w_gu"].shape[1]
    p["w_gu"] = w["w_gu"]
    bgu = w["b_gu"].reshape(L, E, 2 * D_FF // GU_GROUP, LANES, 2)
    p["b_gu"] = jnp.swapaxes(bgu, -1, -2).reshape(L, E, 1, 2 * D_FF)
    p["w_down"] = w["w_down"]
    p["b_down"] = w["b_down"].reshape(L, E, 1, D_MODEL)
    return p


def _rope_tables(positions):
    inv_freq = 1.0 / (ROPE_THETA ** (jnp.arange(0, QK_ROPE, 2, dtype=_f32) / QK_ROPE))
    ang = positions.astype(_f32)[..., None] * inv_freq
    cos, sin = jnp.cos(ang), jnp.sin(ang)
    zeros = jnp.zeros(cos.shape[:-1] + (LANES - QK_ROPE,), _f32)
    return (jnp.concatenate([cos, cos, zeros], axis=-1), jnp.concatenate([-sin, sin, zeros], axis=-1))


def _routing_schedule(cnt_f, n_tokens):
    n = cnt_f[:, 0:N_EXPERTS].astype(jnp.int32)
    counts = jnp.sum(n, axis=0)
    spare = ROW_CHUNK - 1
    padded = jnp.where(counts > 0, (counts + spare + EXPERT_BLOCK - 1) // EXPERT_BLOCK * EXPERT_BLOCK, 0)
    pend = jnp.cumsum(padded)
    pstart = pend - padded
    base = pstart[None, :] + jnp.cumsum(n, axis=0) - n
    n_chunks = (n + ROW_CHUNK - 1) // ROW_CHUNK
    max_rows = n_tokens * TOP_K + N_EXPERTS * (spare + EXPERT_BLOCK - 1) + ROW_CHUNK
    n_blocks = -(-max_rows // EXPERT_BLOCK)
    blk_row = jnp.arange(n_blocks, dtype=jnp.int32) * EXPERT_BLOCK
    blk_exp = jnp.minimum(jnp.sum((blk_row[:, None] >= pend[None, :]).astype(jnp.int32), axis=1), N_EXPERTS - 1)
    onehot = (blk_exp[:, None] == jnp.arange(N_EXPERTS, dtype=jnp.int32)[None, :]).astype(jnp.int32)
    seg_end = jnp.sum(onehot * (pstart + counts)[None, :], axis=1)
    blk_valid = jnp.where(blk_row < pend[-1], jnp.clip(seg_end - blk_row, 0, EXPERT_BLOCK), 0)
    i32 = lambda a: a.astype(jnp.int32)
    n_rows = n_blocks * EXPERT_BLOCK
    seg_chunks = (padded - counts + ROW_CHUNK - 1) // ROW_CHUNK
    pad_start = jnp.concatenate([pend - seg_chunks * ROW_CHUNK, pend[-1:]])
    pad_chunks = jnp.concatenate([seg_chunks, (n_rows + DUMP_ROWS - pend[-1:]) // ROW_CHUNK])
    cum_end = jnp.cumsum(n_chunks, axis=1)
    cum_start = cum_end - n_chunks
    c_idx = jnp.arange(MAX_TILE_CHUNKS, dtype=jnp.int32)
    e_of = jnp.minimum(jnp.sum((c_idx[None, :, None] >= cum_end[:, None, :]).astype(jnp.int32), axis=-1),
                       N_EXPERTS - 1)
    pick = (e_of[:, :, None] == jnp.arange(N_EXPERTS, dtype=jnp.int32)[None, None, :]).astype(jnp.int32)
    chunk_dst = (jnp.sum(pick * base[:, None, :], axis=-1)
                 + (c_idx[None, :] - jnp.sum(pick * cum_start[:, None, :], axis=-1)) * ROW_CHUNK)
    needed = c_idx[None, :] < cum_end[:, -1:]
    write_rows = jnp.where(needed, chunk_dst, n_rows + c_idx[None, :] * ROW_CHUNK)
    read_rows = jnp.where(needed, chunk_dst, 0)
    sched = dict(write_rows=i32(write_rows).reshape(-1), read_rows=i32(read_rows).reshape(-1),
                 n_total=i32(cum_end[:, -1]),
                 pad_start=i32(pad_start), pad_chunks=i32(pad_chunks))
    return sched, i32(blk_exp), i32(blk_valid), n_rows


def kernel(x, positions, attn_norm_g, w_in, pool_w, pool_scale, conv_dw, conv_dw_b, conv_ln_g, conv_ln_b, conv_pw, conv_pw_b, q_a_g, w_uq, kv_a_g, w_ukv, q_norm_g, k_norm_g, pool_out_g, conv_out_g, attn_out_g, w_o, ffn_norm_g, w_router, b_router, w_gu, b_gu, w_down, b_down):
    B, S, D = x.shape
    T = B * S
    depth = w_in.shape[0]
    p = _prepare_params(dict(
        attn_norm_g=attn_norm_g, w_in=w_in, pool_w=pool_w, pool_scale=pool_scale, conv_dw=conv_dw,
        conv_dw_b=conv_dw_b, conv_ln_g=conv_ln_g, conv_ln_b=conv_ln_b, conv_pw=conv_pw, conv_pw_b=conv_pw_b,
        q_a_g=q_a_g, w_uq=w_uq, kv_a_g=kv_a_g, w_ukv=w_ukv, q_norm_g=q_norm_g, k_norm_g=k_norm_g,
        pool_out_g=pool_out_g, conv_out_g=conv_out_g, attn_out_g=attn_out_g, w_o=w_o, ffn_norm_g=ffn_norm_g,
        w_router=w_router, b_router=b_router, w_gu=w_gu, b_gu=b_gu, w_down=w_down, b_down=b_down))
    rope_c, rope_s = _rope_tables(positions)
    for layer in range(depth):
        mix, q, k, v = _mixer_front(layer, x, rope_c, rope_s, p)
        att = _attention(q, k, v)
        x1, h2, route, rgate, cnt = _post_mix(layer, x.reshape(T, D), mix.reshape(T, -1), att.reshape(T, -1), p)
        sched, blk_exp, blk_valid, n_rows = _routing_schedule(cnt, T)
        slots_t = route[:, TOP_K:2 * TOP_K].reshape(T // ROUTE_TILE, ROUTE_TILE, TOP_K).transpose(0, 2, 1)
        slots_t = jnp.pad(slots_t, ((0, 0), (0, SUBLANES - TOP_K), (0, 0)), constant_values=-1)
        xs = _dispatch(h2, slots_t, sched, n_rows)
        ys = _experts(layer, xs, n_rows, blk_exp, blk_valid, p["w_gu"], p["b_gu"], p["w_down"], p["b_down"])
        x = _combine(x1, route, rgate, ys, sched).reshape(B, S, D)
    return x
```

```python
import functools
import math

import jax
import jax.numpy as jnp
from jax import lax
from jax.experimental import pallas as pl
from jax.experimental.pallas import tpu as pltpu

D_MODEL = 1024
CHUNK = 64
POOL_W = 256
POOL_WINDOWS = (2, 4, 8, 16)
POOL_GC = 64
CONV_W = 256
CONV_K = 31
N_HEADS = 4
QK_NOPE = 128
QK_ROPE = 64
QK_HEAD = QK_NOPE + QK_ROPE
V_DIM = 128
Q_LORA = 256
KV_LORA = 128
ATTN_W = N_HEADS * V_DIM
ROPE_THETA = 10000.0
N_EXPERTS = 32
TOP_K = 4
D_FF = D_MODEL
SWIGLU_LIMIT = 7.0
SWIGLU_ALPHA = 1.702
EPS = 1e-6

LANES = 128
HEAD_PAD = 2 * LANES
D_IN_PAD = 1280
POOL_HALO = 16
CONV_HALO = 32
NEG_BIG = -1e30

TM_FRONT = 512
TQ = 512
TM_POST = 1024
ROUTE_TILE = 128
ROW_CHUNK = 8
SUBLANES = 8
STAGE_ROWS = -(-(ROUTE_TILE * TOP_K + N_EXPERTS * (ROW_CHUNK - 1)) // LANES) * LANES
EXPERT_BLOCK = 512
FF_CHUNK = 512
VMEM_LIMIT = 56 * 1024 * 1024

_bf16 = jnp.bfloat16
_f32 = jnp.float32


def _dot(a, b):
    return jnp.dot(a, b, preferred_element_type=_f32)


def _rms(x, width):
    return lax.rsqrt(jnp.sum(x * x, axis=-1, keepdims=True) * (1.0 / width) + EPS)


def _layer_spec(shape, layer, n_grid):
    nd = len(shape)
    block = (None,) + tuple(shape[1:])
    if n_grid == 1:
        return pl.BlockSpec(block, lambda i: (layer,) + (0,) * (nd - 1))
    return pl.BlockSpec(block, lambda i, j: (layer,) + (0,) * (nd - 1))


def _rope(x, c_tab, s_tab):
    lane = lax.broadcasted_iota(jnp.int32, x.shape, 1)
    partner = jnp.where(lane < QK_ROPE // 2, pltpu.roll(x, LANES - QK_ROPE // 2, 1),
                        pltpu.roll(x, QK_ROPE // 2, 1))
    return x * c_tab + partner * s_tab


def _front_kernel(x_ref, ropec_ref, ropes_ref, ang_ref, win_ref, poolw_ref, pscale_ref, cdw_ref, cdwb_ref,
                  clng_ref, clnb_ref, cpw_ref, cpwb_ref, qag_ref, wuq_ref, kvag_ref, wukv_ref, qng_ref,
                  kng_ref, poutg_ref, coutg_ref,
                  mix_ref, q_ref, k_ref, v_ref, pool_ext, conv_ext):
    j = pl.program_id(1)
    tm = x_ref.shape[0]

    x = x_ref[...]
    h = x * _rms(x, D_MODEL) * ang_ref[...]
    z = _dot(h.astype(_bf16), win_ref[...])

    u = z[:, 0:POOL_W]

    @pl.when(j == 0)
    def _():
        pool_ext[0:POOL_HALO, :] = jnp.zeros((POOL_HALO, POOL_W), _f32)
        conv_ext[0:CONV_HALO, :] = jnp.zeros((CONV_HALO, CONV_W), _f32)

    @pl.when(j > 0)
    def _():
        pool_ext[0:POOL_HALO, :] = pool_ext[tm:tm + POOL_HALO, :]
        conv_ext[0:CONV_HALO, :] = conv_ext[tm:tm + CONV_HALO, :]

    pool_ext[POOL_HALO:POOL_HALO + tm, :] = u
    t_pos = (j * tm + lax.broadcasted_iota(jnp.int32, (tm, LANES), 0) + 1).astype(_f32)
    lane = lax.broadcasted_iota(jnp.int32, (tm, LANES), 1)
    pooled_halves = []
    for half, (w_lo, w_hi) in enumerate(((POOL_WINDOWS[0], POOL_WINDOWS[1]),
                                         (POOL_WINDOWS[2], POOL_WINDOWS[3]))):
        cols = slice(half * LANES, (half + 1) * LANES)
        s_lo = pool_ext[POOL_HALO:POOL_HALO + tm, cols]
        for d in range(1, w_lo):
            s_lo = s_lo + pool_ext[POOL_HALO - d:POOL_HALO - d + tm, cols]
        s_hi = s_lo
        for d in range(w_lo, w_hi):
            s_hi = s_hi + pool_ext[POOL_HALO - d:POOL_HALO - d + tm, cols]
        first = lane < POOL_GC
        win_sum = jnp.where(first, s_lo, s_hi)
        cnt = jnp.minimum(t_pos, jnp.where(first, float(w_lo), float(w_hi)))
        pooled_halves.append(win_sum / cnt - u[:, cols])
    pooled = jnp.concatenate(pooled_halves, axis=-1)
    y_pool = _dot(pooled.astype(_bf16), poolw_ref[...]) * pscale_ref[...]
    y_pool = y_pool * _rms(y_pool, POOL_W) * poutg_ref[...]

    a = z[:, POOL_W:POOL_W + CONV_W]
    gate = z[:, POOL_W + CONV_W:POOL_W + 2 * CONV_W]
    conv_ext[CONV_HALO:CONV_HALO + tm, :] = a * jax.nn.sigmoid(gate)
    base = CONV_HALO - (CONV_K - 1)
    acc = cdwb_ref[...]
    window = conv_ext[...]
    n_win = window.shape[0]
    for r in range(SUBLANES):
        tiles = [(base + kk) // SUBLANES for kk in range(CONV_K) if (base + kk) % SUBLANES == r]
        shifted = window if r == 0 else pltpu.roll(window, n_win - r, 0)
        for m in tiles:
            kk = m * SUBLANES + r - base
            acc = acc + shifted[m * SUBLANES:m * SUBLANES + tm, :] * cdw_ref[kk:kk + 1, :]
    mu = jnp.mean(acc, axis=-1, keepdims=True)
    cen = acc - mu
    var = jnp.mean(cen * cen, axis=-1, keepdims=True)
    ln = cen * lax.rsqrt(var + EPS) * clng_ref[...] + clnb_ref[...]
    sw = ln * jax.nn.sigmoid(ln)
    y_conv = _dot(sw.astype(_bf16), cpw_ref[...]) + cpwb_ref[...]
    y_conv = y_conv * _rms(y_conv, CONV_W) * coutg_ref[...]

    mix_ref[:, 0:POOL_W] = y_pool.astype(_bf16)
    mix_ref[:, POOL_W:POOL_W + CONV_W] = y_conv.astype(_bf16)

    s1 = POOL_W + 2 * CONV_W
    c_q = z[:, s1:s1 + Q_LORA]
    c_kv = z[:, s1 + Q_LORA:s1 + Q_LORA + KV_LORA]
    k_rope = z[:, s1 + Q_LORA + KV_LORA:s1 + Q_LORA + KV_LORA + LANES]
    c_tab = ropec_ref[...]
    s_tab = ropes_ref[...]

    q_all = _dot((c_q * _rms(c_q, Q_LORA) * qag_ref[...]).astype(_bf16), wuq_ref[...])
    kv_all = _dot((c_kv * _rms(c_kv, KV_LORA) * kvag_ref[...]).astype(_bf16), wukv_ref[...])
    sm_scale = math.log2(math.e) / math.sqrt(QK_HEAD)
    kr_ss = jnp.sum(k_rope * k_rope, axis=-1, keepdims=True)
    kr_rot = _rope(k_rope * kng_ref[:, LANES:2 * LANES], c_tab, s_tab)
    for hd in range(N_HEADS):
        qh = q_all[:, hd * HEAD_PAD:(hd + 1) * HEAD_PAD]
        qh = qh * (_rms(qh, QK_HEAD) * sm_scale) * qng_ref[...]
        q_ref[:, hd * HEAD_PAD:hd * HEAD_PAD + LANES] = qh[:, 0:LANES].astype(_bf16)
        q_ref[:, hd * HEAD_PAD + LANES:(hd + 1) * HEAD_PAD] = _rope(qh[:, LANES:], c_tab, s_tab).astype(_bf16)
        kn = kv_all[:, hd * QK_NOPE:(hd + 1) * QK_NOPE]
        k_rs = lax.rsqrt((jnp.sum(kn * kn, axis=-1, keepdims=True) + kr_ss) * (1.0 / QK_HEAD) + EPS)
        k_ref[:, hd * HEAD_PAD:hd * HEAD_PAD + LANES] = (kn * k_rs * kng_ref[:, 0:LANES]).astype(_bf16)
        k_ref[:, hd * HEAD_PAD + LANES:(hd + 1) * HEAD_PAD] = (kr_rot * k_rs).astype(_bf16)
        v_lo = hd * (V_DIM + LANES)
        v_ref[:, v_lo:v_lo + V_DIM] = kv_all[:, (N_HEADS + hd) * V_DIM:(N_HEADS + hd + 1) * V_DIM].astype(_bf16)
        v_ref[:, v_lo + V_DIM:v_lo + V_DIM + LANES] = jnp.ones((tm, LANES), _bf16)


def _mixer_front(layer, x, rope_c, rope_s, p):
    B, S, _ = x.shape
    tm = TM_FRONT
    row = lambda w: pl.BlockSpec((None, tm, w), lambda b, j: (b, j, 0))
    names = ("attn_norm_g", "w_in", "pool_w", "pool_scale", "conv_dw", "conv_dw_b", "conv_ln_g", "conv_ln_b",
             "conv_pw", "conv_pw_b", "q_a_g", "w_uq", "kv_a_g", "w_ukv", "q_norm_g", "k_norm_g",
             "pool_out_g", "conv_out_g")
    weights = [p[n] for n in names]
    in_specs = [row(D_MODEL), row(LANES), row(LANES)] + [_layer_spec(w.shape, layer, 2) for w in weights]
    out_shape = (jax.ShapeDtypeStruct((B, S, POOL_W + CONV_W), _bf16),
                 jax.ShapeDtypeStruct((B, S, N_HEADS * HEAD_PAD), _bf16),
                 jax.ShapeDtypeStruct((B, S, N_HEADS * HEAD_PAD), _bf16),
                 jax.ShapeDtypeStruct((B, S, N_HEADS * (V_DIM + LANES)), _bf16))
    out_specs = (row(POOL_W + CONV_W), row(N_HEADS * HEAD_PAD), row(N_HEADS * HEAD_PAD),
                 row(N_HEADS * (V_DIM + LANES)))
    return pl.pallas_call(
        _front_kernel,
        grid=(B, S // tm),
        in_specs=in_specs,
        out_specs=out_specs,
        out_shape=out_shape,
        scratch_shapes=[pltpu.VMEM((tm + POOL_HALO, POOL_W), _f32),
                        pltpu.VMEM((tm + CONV_HALO, CONV_W), _f32)],
        compiler_params=pltpu.CompilerParams(dimension_semantics=("arbitrary", "arbitrary"),
                                             vmem_limit_bytes=VMEM_LIMIT),
        name="mixer_front",
    )(x, rope_c, rope_s, *weights)


def _attn_kernel(q_ref, qnext_ref, k_ref, v_ref, o_ref, s_scr, m_scr):
    qi = pl.program_id(2)
    tq = q_ref.shape[0]
    n_tiles = k_ref.shape[0] // tq
    q_chunk = lax.broadcasted_iota(jnp.int32, (tq, tq), 0) // CHUNK
    k_chunk = lax.broadcasted_iota(jnp.int32, (tq, tq), 1) // CHUNK

    def score_phase(n, q_tile_ref):
        keys = (n + 1) * tq
        s = lax.dot_general(q_tile_ref[...], k_ref[0:keys, :], (((1,), (1,)), ((), ())),
                            preferred_element_type=_f32)
        diag = jnp.where(k_chunk <= q_chunk, s[:, n * tq:], NEG_BIG)
        s = diag if n == 0 else jnp.concatenate([s[:, :n * tq], diag], axis=-1)
        m_scr[n % 2] = jnp.broadcast_to(jnp.max(s, axis=-1, keepdims=True), (tq, LANES))
        s_scr[n % 2, :, 0:keys] = s

    def value_phase(n):
        keys = (n + 1) * tq
        p = jnp.exp2(s_scr[n % 2, :, 0:keys] - m_scr[n % 2, :, 0:1]).astype(_bf16)
        acc = _dot(p, v_ref[0:keys, :])
        o_ref[...] = (acc[:, 0:V_DIM] / acc[:, V_DIM:V_DIM + 1]).astype(o_ref.dtype)

    def step(n):
        if n == 0:
            score_phase(0, q_ref)
        value_phase(n)
        if n + 1 < n_tiles:
            score_phase(n + 1, qnext_ref)

    for n in range(n_tiles):
        pl.when(qi == n)(functools.partial(step, n))


def _attention(q, k, v):
    B, S, _ = q.shape
    nq = S // TQ
    return pl.pallas_call(
        _attn_kernel,
        grid=(B, N_HEADS, nq),
        in_specs=[pl.BlockSpec((None, TQ, HEAD_PAD), lambda b, h, i: (b, i, h)),
                  pl.BlockSpec((None, TQ, HEAD_PAD), lambda b, h, i: (b, jnp.minimum(i + 1, nq - 1), h)),
                  pl.BlockSpec((None, S, HEAD_PAD), lambda b, h, i: (b, 0, h)),
                  pl.BlockSpec((None, S, V_DIM + LANES), lambda b, h, i: (b, 0, h))],
        out_specs=pl.BlockSpec((None, TQ, V_DIM), lambda b, h, i: (b, i, h)),
        out_shape=jax.ShapeDtypeStruct((B, S, ATTN_W), _bf16),
        scratch_shapes=[pltpu.VMEM((2, TQ, S), _f32), pltpu.VMEM((2, TQ, LANES), _f32)],
        compiler_params=pltpu.CompilerParams(dimension_semantics=("arbitrary", "arbitrary", "arbitrary"),
                                             vmem_limit_bytes=VMEM_LIMIT),
        name="attention",
    )(q, q, k, v)


def _post_kernel(x_ref, mix_ref, att_ref, aog_ref, wo_ref, fng_ref, wr_ref, br_ref,
                 x1_ref, h2_ref, route_ref, rgate_ref, cnt_ref):
    tm = x_ref.shape[0]
    att = att_ref[...].astype(_f32)
    att_n = (att * _rms(att, ATTN_W) * aog_ref[...]).astype(_bf16)
    x1 = x_ref[...] + _dot(mix_ref[...], wo_ref[0:POOL_W + CONV_W, :]) + _dot(att_n, wo_ref[POOL_W + CONV_W:, :])
    x1_ref[...] = x1
    h2 = (x1 * _rms(x1, D_MODEL) * fng_ref[...]).astype(_bf16)
    h2_ref[...] = h2

    logits = _dot(h2, wr_ref[...]) + br_ref[...]
    lane = lax.broadcasted_iota(jnp.int32, logits.shape, 1)
    lane_f = lane.astype(_f32)
    work = logits
    sel = jnp.zeros(logits.shape, _f32)
    onehots, vals, idxs = [], [], []
    for _ in range(TOP_K):
        mx = jnp.max(work, axis=-1, keepdims=True)
        idx_f = jnp.min(jnp.where(work == mx, lane_f, float(LANES)), axis=-1, keepdims=True)
        idx = idx_f.astype(jnp.int32)
        hot = lane == idx
        onehots.append(hot)
        vals.append(mx)
        idxs.append(idx)
        sel = jnp.where(hot, 1.0, sel)
        work = jnp.where(hot, -jnp.inf, work)
    exps = [jnp.exp(vk - vals[0]) for vk in vals]
    denom = exps[0] + exps[1] + exps[2] + exps[3]

    nt = tm // ROUTE_TILE
    r_i = lax.broadcasted_iota(jnp.int32, (ROUTE_TILE, ROUTE_TILE), 0)
    c_i = lax.broadcasted_iota(jnp.int32, (ROUTE_TILE, ROUTE_TILE), 1)
    tri = jnp.where(c_i < r_i, 1.0, 0.0).astype(_bf16)
    upper = jnp.where(r_i < c_i, 1.0, 0.0).astype(_bf16)
    sel_b = sel.astype(_bf16)
    ranks, cnts = [], []
    for s in range(nt):
        rows = slice(s * ROUTE_TILE, (s + 1) * ROUTE_TILE)
        ranks.append(_dot(tri, sel_b[rows, :]))
        cnts.append(jnp.sum(sel[rows, :], axis=0, keepdims=True))
    cnt = jnp.concatenate(cnts, axis=0)
    cnt_ref[...] = cnt
    chunks = jnp.floor((cnt + (ROW_CHUNK - 1)) * (1.0 / ROW_CHUNK))
    boff = _dot(chunks.astype(_bf16), upper) * float(ROW_CHUNK)
    slot = jnp.concatenate(
        [ranks[s] + jnp.broadcast_to(boff[s:s + 1, :], (ROUTE_TILE, LANES)) for s in range(nt)], axis=0)

    route = jnp.zeros(logits.shape, jnp.int32)
    rgate = jnp.zeros(logits.shape, _f32)
    for kk in range(TOP_K):
        col = jnp.sum(jnp.where(onehots[kk], slot, 0.0), axis=-1, keepdims=True).astype(jnp.int32)
        route = jnp.where(lane == kk, idxs[kk], route)
        route = jnp.where(lane == TOP_K + kk, col, route)
        rgate = jnp.where(lane == kk, exps[kk] / denom, rgate)
    route_ref[...] = route
    rgate_ref[...] = rgate


def _post_mix(layer, x, mix, att, p):
    T = x.shape[0]
    tm = TM_POST
    row = lambda w: pl.BlockSpec((tm, w), lambda i: (i, 0))
    names = ("attn_out_g", "w_o", "ffn_norm_g", "w_router", "b_router")
    weights = [p[n] for n in names]
    out_shape = (jax.ShapeDtypeStruct((T, D_MODEL), _f32), jax.ShapeDtypeStruct((T, D_MODEL), _bf16),
                 jax.ShapeDtypeStruct((T, LANES), jnp.int32), jax.ShapeDtypeStruct((T, LANES), _f32),
                 jax.ShapeDtypeStruct((T // ROUTE_TILE, LANES), _f32))
    return pl.pallas_call(
        _post_kernel,
        grid=(T // tm,),
        in_specs=[row(D_MODEL), row(POOL_W + CONV_W), row(ATTN_W)] + [_layer_spec(w.shape, layer, 1) for w in weights],
        out_specs=(row(D_MODEL), row(D_MODEL), row(LANES), row(LANES),
                   pl.BlockSpec((tm // ROUTE_TILE, LANES), lambda i: (i, 0))),
        out_shape=out_shape,
        compiler_params=pltpu.CompilerParams(dimension_semantics=("arbitrary",), vmem_limit_bytes=VMEM_LIMIT),
        name="post_mix",
    )(x, mix, att, *weights)


N_SLAB = D_MODEL // LANES
assert N_SLAB == SUBLANES
CHUNK_SUBROWS = ROW_CHUNK * N_SLAB


def _to_token_major(ref, value):
    rows = value.shape[0]
    for j in range(N_SLAB):
        ref[pl.ds(j, rows, stride=N_SLAB), :] = value[:, j * LANES:(j + 1) * LANES]


def _from_token_major(ref, rows):
    return jnp.concatenate([ref[pl.ds(j, rows, stride=N_SLAB), :] for j in range(N_SLAB)], axis=-1)


def _subrow(row):
    return pl.multiple_of(row * N_SLAB, N_SLAB)


MAX_TILE_CHUNKS = ROUTE_TILE * TOP_K // ROW_CHUNK + N_EXPERTS
assert MAX_TILE_CHUNKS * ROW_CHUNK == STAGE_ROWS
DUMP_ROWS = STAGE_ROWS


ALWAYS_CHUNKS = ROUTE_TILE * TOP_K // ROW_CHUNK + N_EXPERTS // 4
CHUNK_GROUP = 8
assert (MAX_TILE_CHUNKS - ALWAYS_CHUNKS) % CHUNK_GROUP == 0


def _tile_chunks(chunk_rows, n_total, tile, chunk_fn):
    def run(lo, hi):
        for c in range(lo, hi):
            chunk_fn(c, c * ROW_CHUNK, chunk_rows[tile * MAX_TILE_CHUNKS + c])

    run(0, ALWAYS_CHUNKS)
    for lo in range(ALWAYS_CHUNKS, MAX_TILE_CHUNKS, CHUNK_GROUP):
        pl.when(n_total[tile] > lo)(functools.partial(run, lo, lo + CHUNK_GROUP))


def _dispatch_kernel(chunk_rows, n_total, pad_start, pad_chunks,
                     slot_ref, h2_ref, xs_hbm, stage, zeros, sems, zsem):
    i = pl.program_id(0)
    n_steps = pl.num_programs(0)
    cur = i % 2

    def chunk_copy(slot, stage_row, sorted_row):
        return pltpu.make_async_copy(stage.at[slot, pl.ds(_subrow(stage_row), CHUNK_SUBROWS)],
                                     xs_hbm.at[pl.ds(_subrow(sorted_row), CHUNK_SUBROWS)], sems.at[slot])

    def send_tile(tile, slot):
        _tile_chunks(chunk_rows, n_total, tile, lambda c, sr, dr: chunk_copy(slot, sr, dr).start(priority=c % 2))

    def wait_tile(tile, slot):
        _tile_chunks(chunk_rows, n_total, tile, lambda c, sr, dr: chunk_copy(slot, 0, 0).wait())

    @pl.when(i == 0)
    def _():
        zeros[...] = jnp.zeros(zeros.shape, _f32)

        def zero_copy(row):
            return pltpu.make_async_copy(zeros, xs_hbm.at[pl.ds(_subrow(row), CHUNK_SUBROWS)], zsem)

        for e in range(N_EXPERTS + 1):
            def start(c, carry, e=e):
                zero_copy(pad_start[e] + c * ROW_CHUNK).start()
                return carry
            lax.fori_loop(0, pad_chunks[e], start, 0)
        for e in range(N_EXPERTS + 1):
            def wait(c, carry, e=e):
                zero_copy(pad_start[e] + c * ROW_CHUNK).wait()
                return carry
            lax.fori_loop(0, pad_chunks[e], wait, 0)

    slots = slot_ref[...]
    row_i = lax.broadcasted_iota(jnp.int32, (STAGE_ROWS, ROUTE_TILE), 0)
    place = jnp.zeros((STAGE_ROWS, ROUTE_TILE), _f32)
    for kk in range(TOP_K):
        place = place + jnp.where(row_i == slots[kk:kk + 1, :], 1.0, 0.0)
    staged = _dot(place.astype(_bf16), h2_ref[...])
    _to_token_major(stage.at[cur], staged)

    @pl.when(i > 0)
    def _():
        wait_tile(i - 1, 1 - cur)

    send_tile(i, cur)

    @pl.when(i == n_steps - 1)
    def _():
        wait_tile(i, cur)


def _dispatch(h2, slots_t, sched, n_rows):
    T = h2.shape[0]
    grid_spec = pltpu.PrefetchScalarGridSpec(
        num_scalar_prefetch=4,
        grid=(T // ROUTE_TILE,),
        in_specs=[pl.BlockSpec((None, SUBLANES, ROUTE_TILE), lambda i, *_: (i, 0, 0)),
                  pl.BlockSpec((ROUTE_TILE, D_MODEL), lambda i, *_: (i, 0))],
        out_specs=pl.BlockSpec(memory_space=pl.ANY),
        scratch_shapes=[pltpu.VMEM((2, STAGE_ROWS * N_SLAB, LANES), _f32),
                        pltpu.VMEM((CHUNK_SUBROWS, LANES), _f32),
                        pltpu.SemaphoreType.DMA((2,)), pltpu.SemaphoreType.DMA(())],
    )
    return pl.pallas_call(
        _dispatch_kernel,
        grid_spec=grid_spec,
        out_shape=jax.ShapeDtypeStruct(((n_rows + DUMP_ROWS) * N_SLAB, LANES), _f32),
        compiler_params=pltpu.CompilerParams(dimension_semantics=("arbitrary",), vmem_limit_bytes=VMEM_LIMIT),
        name="dispatch",
    )(sched["write_rows"], sched["n_total"], sched["pad_start"], sched["pad_chunks"], slots_t, h2)


GU_GROUP = 2 * LANES


def _expert_kernel(blk_exp, blk_valid, xs_ref, wgu_ref, bgu_ref, wd_ref, bd_ref, y_ref, wgu_s, wd_s):
    i = pl.program_id(0)
    valid = blk_valid[i]

    @pl.when(jnp.logical_or(i == 0, blk_exp[i] != blk_exp[jnp.maximum(i - 1, 0)]))
    def _():
        r_i = lax.broadcasted_iota(jnp.int32, (GU_GROUP, GU_GROUP), 0)
        c_i = lax.broadcasted_iota(jnp.int32, (GU_GROUP, GU_GROUP), 1)
        src = jnp.where(c_i < LANES, 2 * c_i, 2 * (c_i - LANES) + 1)
        perm = jnp.where(r_i == src, 1.0, 0.0).astype(_bf16)
        for b in range(2 * D_FF // GU_GROUP):
            cols = slice(b * GU_GROUP, (b + 1) * GU_GROUP)
            wgu_s[:, cols] = _dot(wgu_ref[:, cols].astype(_bf16), perm).astype(_bf16)
        wd_s[...] = wd_ref[...].astype(_bf16)

    @pl.when(valid > 0)
    def _():
        x = _from_token_major(xs_ref, EXPERT_BLOCK).astype(_bf16)
        y = jnp.zeros((EXPERT_BLOCK, D_MODEL), _f32)
        groups = FF_CHUNK // LANES
        for c in range(D_FF // FF_CHUNK):
            lo = 2 * c * FF_CHUNK
            h = _dot(x, wgu_s[:, lo:lo + 2 * FF_CHUNK]) + bgu_ref[:, lo:lo + 2 * FF_CHUNK]
            hg = jnp.concatenate([h[:, (2 * q) * LANES:(2 * q + 1) * LANES] for q in range(groups)], axis=-1)
            hl = jnp.concatenate([h[:, (2 * q + 1) * LANES:(2 * q + 2) * LANES] for q in range(groups)], axis=-1)
            g = jnp.minimum(hg, SWIGLU_LIMIT)
            lin = jnp.clip(hl, -SWIGLU_LIMIT, SWIGLU_LIMIT)
            act = g * jax.nn.sigmoid(SWIGLU_ALPHA * g) * (lin + 1.0)
            y = y + _dot(act.astype(_bf16), wd_s[c * FF_CHUNK:(c + 1) * FF_CHUNK, :])
        _to_token_major(y_ref, y + bd_ref[...])

    @pl.when(valid == 0)
    def _():
        y_ref[...] = jnp.zeros(y_ref.shape, _f32)


def _experts(layer, xs, n_rows, blk_exp, blk_valid, w_gu, b_gu, w_down, b_down):
    nb = n_rows // EXPERT_BLOCK
    grid_spec = pltpu.PrefetchScalarGridSpec(
        num_scalar_prefetch=2,
        grid=(nb,),
        in_specs=[pl.BlockSpec((EXPERT_BLOCK * N_SLAB, LANES), lambda i, be, bv: (i, 0)),
                  pl.BlockSpec((None, None, D_MODEL, 2 * D_FF), lambda i, be, bv: (layer, be[i], 0, 0)),
                  pl.BlockSpec((None, None, 1, 2 * D_FF), lambda i, be, bv: (layer, be[i], 0, 0)),
                  pl.BlockSpec((None, None, D_FF, D_MODEL), lambda i, be, bv: (layer, be[i], 0, 0)),
                  pl.BlockSpec((None, None, 1, D_MODEL), lambda i, be, bv: (layer, be[i], 0, 0))],
        out_specs=pl.BlockSpec((EXPERT_BLOCK * N_SLAB, LANES), lambda i, be, bv: (i, 0)),
        scratch_shapes=[pltpu.VMEM((D_MODEL, 2 * D_FF), _bf16), pltpu.VMEM((D_FF, D_MODEL), _bf16)],
    )
    return pl.pallas_call(
        _expert_kernel,
        grid_spec=grid_spec,
        out_shape=jax.ShapeDtypeStruct((n_rows * N_SLAB, LANES), _f32),
        compiler_params=pltpu.CompilerParams(dimension_semantics=("arbitrary",), vmem_limit_bytes=VMEM_LIMIT),
        name="experts",
    )(blk_exp, blk_valid, xs, w_gu, b_gu, w_down, b_down)


def _combine_kernel(chunk_rows, n_total, x1_ref, route_ref, gate_ref, ys_hbm, out_ref, stage, sems):
    i = pl.program_id(0)
    n_steps = pl.num_programs(0)
    cur = i % 2

    def chunk_copy(slot, stage_row, sorted_row):
        return pltpu.make_async_copy(ys_hbm.at[pl.ds(_subrow(sorted_row), CHUNK_SUBROWS)],
                                     stage.at[slot, pl.ds(_subrow(stage_row), CHUNK_SUBROWS)], sems.at[slot])

    def fetch_tile(tile, slot):
        _tile_chunks(chunk_rows, n_total, tile, lambda c, sr, dr: chunk_copy(slot, sr, dr).start(priority=c % 2))

    def wait_tile(tile, slot):
        _tile_chunks(chunk_rows, n_total, tile, lambda c, sr, dr: chunk_copy(slot, 0, 0).wait())

    @pl.when(i == 0)
    def _():
        stage[...] = jnp.zeros(stage.shape, _f32)
        fetch_tile(0, 0)

    @pl.when(i + 1 < n_steps)
    def _():
        fetch_tile(i + 1, 1 - cur)

    wait_tile(i, cur)
    rows = _from_token_major(stage.at[cur], STAGE_ROWS).astype(_bf16)
    route = route_ref[...]
    gates = gate_ref[...]
    lane_i = lax.broadcasted_iota(jnp.int32, (ROUTE_TILE, STAGE_ROWS), 1)
    weight = jnp.zeros((ROUTE_TILE, STAGE_ROWS), _f32)
    for kk in range(TOP_K):
        weight = weight + jnp.where(lane_i == route[:, TOP_K + kk:TOP_K + kk + 1], gates[:, kk:kk + 1], 0.0)
    out_ref[...] = x1_ref[...] + _dot(weight.astype(_bf16), rows)


def _combine(x1, route, rgate, ys, sched):
    T = x1.shape[0]
    tc = ROUTE_TILE
    grid_spec = pltpu.PrefetchScalarGridSpec(
        num_scalar_prefetch=2,
        grid=(T // tc,),
        in_specs=[pl.BlockSpec((tc, D_MODEL), lambda i, *_: (i, 0)),
                  pl.BlockSpec((tc, LANES), lambda i, *_: (i, 0)),
                  pl.BlockSpec((tc, LANES), lambda i, *_: (i, 0)),
                  pl.BlockSpec(memory_space=pl.ANY)],
        out_specs=pl.BlockSpec((tc, D_MODEL), lambda i, *_: (i, 0)),
        scratch_shapes=[pltpu.VMEM((2, STAGE_ROWS * N_SLAB, LANES), _f32),
                        pltpu.SemaphoreType.DMA((2,))],
    )
    return pl.pallas_call(
        _combine_kernel,
        grid_spec=grid_spec,
        out_shape=jax.ShapeDtypeStruct((T, D_MODEL), _f32),
        compiler_params=pltpu.CompilerParams(dimension_semantics=("arbitrary",), vmem_limit_bytes=VMEM_LIMIT),
        name="combine",
    )(sched["read_rows"], sched["n_total"], x1, route, rgate, ys)


def _prepare_params(w):
    L = w["w_in"].shape[0]
    p = {}
    row = lambda a: a.reshape(L, 1, a.shape[-1])
    p["attn_norm_g"] = row(w["attn_norm_g"])
    p["w_in"] = jnp.pad(w["w_in"], ((0, 0), (0, 0), (0, D_IN_PAD - w["w_in"].shape[-1]))).astype(_bf16)
    eye = jnp.eye(len(POOL_WINDOWS), dtype=_f32)
    p["pool_w"] = jnp.einsum("lgcd,gh->lgchd", w["pool_w"], eye).reshape(L, POOL_W, POOL_W).astype(_bf16)
    p["pool_scale"] = row(w["pool_scale"])
    p["conv_dw"] = w["conv_dw"]
    p["conv_dw_b"] = row(w["conv_dw_b"])
    p["conv_ln_g"] = row(w["conv_ln_g"])
    p["conv_ln_b"] = row(w["conv_ln_b"])
    p["conv_pw"] = w["conv_pw"].astype(_bf16)
    p["conv_pw_b"] = row(w["conv_pw_b"])
    p["q_a_g"] = row(w["q_a_g"])
    wuq = w["w_uq"].reshape(L, Q_LORA, N_HEADS, QK_HEAD)
    p["w_uq"] = jnp.pad(wuq, ((0, 0), (0, 0), (0, 0), (0, HEAD_PAD - QK_HEAD))).reshape(
        L, Q_LORA, N_HEADS * HEAD_PAD).astype(_bf16)
    p["kv_a_g"] = row(w["kv_a_g"])
    wukv = w["w_ukv"].reshape(L, KV_LORA, N_HEADS, QK_NOPE + V_DIM)
    p["w_ukv"] = jnp.concatenate([wukv[..., :QK_NOPE].reshape(L, KV_LORA, N_HEADS * QK_NOPE),
                                  wukv[..., QK_NOPE:].reshape(L, KV_LORA, N_HEADS * V_DIM)], axis=-1).astype(_bf16)
    pad_head = lambda g: jnp.pad(g, ((0, 0), (0, HEAD_PAD - QK_HEAD))).reshape(L, 1, HEAD_PAD)
    p["q_norm_g"] = pad_head(w["q_norm_g"])
    p["k_norm_g"] = pad_head(w["k_norm_g"])
    p["pool_out_g"] = row(w["pool_out_g"])
    p["conv_out_g"] = row(w["conv_out_g"])
    p["attn_out_g"] = row(w["attn_out_g"])
    p["w_o"] = w["w_o"].astype(_bf16)
    p["ffn_norm_g"] = row(w["ffn_norm_g"])
    p["w_router"] = jnp.pad(w["w_router"], ((0, 0), (0, 0), (0, LANES - N_EXPERTS))).astype(_bf16)
    p["b_router"] = jnp.pad(w["b_router"], ((0, 0), (0, LANES - N_EXPERTS)),
                            constant_values=NEG_BIG).reshape(L, 1, LANES)
    E = w["w_gu"].shape[1]
    p["w_gu"] = w["w_gu"]
    bgu = w["b_gu"].reshape(L, E, 2 * D_FF // GU_GROUP, LANES, 2)
    p["b_gu"] = jnp.swapaxes(bgu, -1, -2).reshape(L, E, 1, 2 * D_FF)
    p["w_down"] = w["w_down"]
    p["b_down"] = w["b_down"].reshape(L, E, 1, D_MODEL)
    return p


def _rope_tables(positions):
    inv_freq = 1.0 / (ROPE_THETA ** (jnp.arange(0, QK_ROPE, 2, dtype=_f32) / QK_ROPE))
    ang = positions.astype(_f32)[..., None] * inv_freq
    cos, sin = jnp.cos(ang), jnp.sin(ang)
    zeros = jnp.zeros(cos.shape[:-1] + (LANES - QK_ROPE,), _f32)
    return (jnp.concatenate([cos, cos, zeros], axis=-1), jnp.concatenate([-sin, sin, zeros], axis=-1))


def _routing_schedule(cnt_f, n_tokens):
    n = cnt_f[:, 0:N_EXPERTS].astype(jnp.int32)
    counts = jnp.sum(n, axis=0)
    spare = ROW_CHUNK - 1
    padded = jnp.where(counts > 0, (counts + spare + EXPERT_BLOCK - 1) // EXPERT_BLOCK * EXPERT_BLOCK, 0)
    pend = jnp.cumsum(padded)
    pstart = pend - padded
    base = pstart[None, :] + jnp.cumsum(n, axis=0) - n
    n_chunks = (n + ROW_CHUNK - 1) // ROW_CHUNK
    max_rows = n_tokens * TOP_K + N_EXPERTS * (spare + EXPERT_BLOCK - 1) + ROW_CHUNK
    n_blocks = -(-max_rows // EXPERT_BLOCK)
    blk_row = jnp.arange(n_blocks, dtype=jnp.int32) * EXPERT_BLOCK
    blk_exp = jnp.minimum(jnp.sum((blk_row[:, None] >= pend[None, :]).astype(jnp.int32), axis=1), N_EXPERTS - 1)
    onehot = (blk_exp[:, None] == jnp.arange(N_EXPERTS, dtype=jnp.int32)[None, :]).astype(jnp.int32)
    seg_end = jnp.sum(onehot * (pstart + counts)[None, :], axis=1)
    blk_valid = jnp.where(blk_row < pend[-1], jnp.clip(seg_end - blk_row, 0, EXPERT_BLOCK), 0)
    i32 = lambda a: a.astype(jnp.int32)
    n_rows = n_blocks * EXPERT_BLOCK
    seg_chunks = (padded - counts + ROW_CHUNK - 1) // ROW_CHUNK
    pad_start = jnp.concatenate([pend - seg_chunks * ROW_CHUNK, pend[-1:]])
    pad_chunks = jnp.concatenate([seg_chunks, (n_rows + DUMP_ROWS - pend[-1:]) // ROW_CHUNK])
    cum_end = jnp.cumsum(n_chunks, axis=1)
    cum_start = cum_end - n_chunks
    c_idx = jnp.arange(MAX_TILE_CHUNKS, dtype=jnp.int32)
    e_of = jnp.minimum(jnp.sum((c_idx[None, :, None] >= cum_end[:, None, :]).astype(jnp.int32), axis=-1),
                       N_EXPERTS - 1)
    pick = (e_of[:, :, None] == jnp.arange(N_EXPERTS, dtype=jnp.int32)[None, None, :]).astype(jnp.int32)
    chunk_dst = (jnp.sum(pick * base[:, None, :], axis=-1)
                 + (c_idx[None, :] - jnp.sum(pick * cum_start[:, None, :], axis=-1)) * ROW_CHUNK)
    needed = c_idx[None, :] < cum_end[:, -1:]
    write_rows = jnp.where(needed, chunk_dst, n_rows + c_idx[None, :] * ROW_CHUNK)
    read_rows = jnp.where(needed, chunk_dst, 0)
    sched = dict(write_rows=i32(write_rows).reshape(-1), read_rows=i32(read_rows).reshape(-1),
                 n_total=i32(cum_end[:, -1]),
                 pad_start=i32(pad_start), pad_chunks=i32(pad_chunks))
    return sched, i32(blk_exp), i32(blk_valid), n_rows


def kernel(x, positions, attn_norm_g, w_in, pool_w, pool_scale, conv_dw, conv_dw_b, conv_ln_g, conv_ln_b, conv_pw, conv_pw_b, q_a_g, w_uq, kv_a_g, w_ukv, q_norm_g, k_norm_g, pool_out_g, conv_out_g, attn_out_g, w_o, ffn_norm_g, w_router, b_router, w_gu, b_gu, w_down, b_down):
    B, S, D = x.shape
    T = B * S
    depth = w_in.shape[0]
    p = _prepare_params(dict(
        attn_norm_g=attn_norm_g, w_in=w_in, pool_w=pool_w, pool_scale=pool_scale, conv_dw=conv_dw,
        conv_dw_b=conv_dw_b, conv_ln_g=conv_ln_g, conv_ln_b=conv_ln_b, conv_pw=conv_pw, conv_pw_b=conv_pw_b,
        q_a_g=q_a_g, w_uq=w_uq, kv_a_g=kv_a_g, w_ukv=w_ukv, q_norm_g=q_norm_g, k_norm_g=k_norm_g,
        pool_out_g=pool_out_g, conv_out_g=conv_out_g, attn_out_g=attn_out_g, w_o=w_o, ffn_norm_g=ffn_norm_g,
        w_router=w_router, b_router=b_router, w_gu=w_gu, b_gu=b_gu, w_down=w_down, b_down=b_down))
    rope_c, rope_s = _rope_tables(positions)
    for layer in range(depth):
        mix, q, k, v = _mixer_front(layer, x, rope_c, rope_s, p)
        att = _attention(q, k, v)
        x1, h2, route, rgate, cnt = _post_mix(layer, x.reshape(T, D), mix.reshape(T, -1), att.reshape(T, -1), p)
        sched, blk_exp, blk_valid, n_rows = _routing_schedule(cnt, T)
        slots_t = route[:, TOP_K:2 * TOP_K].reshape(T // ROUTE_TILE, ROUTE_TILE, TOP_K).transpose(0, 2, 1)
        slots_t = jnp.pad(slots_t, ((0, 0), (0, SUBLANES - TOP_K), (0, 0)), constant_values=-1)
        xs = _dispatch(h2, slots_t, sched, n_rows)
        ys = _experts(layer, xs, n_rows, blk_exp, blk_valid, p["w_gu"], p["b_gu"], p["w_down"], p["b_down"])
        x = _combine(x1, route, rgate, ys, sched).reshape(B, S, D)
    return x
```

```python
import functools
import math

import jax
import jax.numpy as jnp
from jax import lax
from jax.experimental import pallas as pl
from jax.experimental.pallas import tpu as pltpu

D_MODEL = 1024
CHUNK = 64
POOL_W = 256
POOL_WINDOWS = (2, 4, 8, 16)
POOL_GC = 64
CONV_W = 256
CONV_K = 31
N_HEADS = 4
QK_NOPE = 128
QK_ROPE = 64
QK_HEAD = QK_NOPE + QK_ROPE
V_DIM = 128
Q_LORA = 256
KV_LORA = 128
ATTN_W = N_HEADS * V_DIM
ROPE_THETA = 10000.0
N_EXPERTS = 32
TOP_K = 4
D_FF = D_MODEL
SWIGLU_LIMIT = 7.0
SWIGLU_ALPHA = 1.702
EPS = 1e-6

LANES = 128
HEAD_PAD = 2 * LANES
D_IN_PAD = 1280
POOL_HALO = 16
CONV_HALO = 32
NEG_BIG = -1e30

TM_FRONT = 512
TQ = 512
TM_POST = 1024
ROUTE_TILE = 256
ROW_CHUNK = 8
SUBLANES = 8
STAGE_ROWS = -(-(ROUTE_TILE * TOP_K + N_EXPERTS * (ROW_CHUNK - 1)) // LANES) * LANES
EXPERT_BLOCK = 512
FF_CHUNK = 512
VMEM_LIMIT = 56 * 1024 * 1024

_bf16 = jnp.bfloat16
_f32 = jnp.float32


def _dot(a, b):
    return jnp.dot(a, b, preferred_element_type=_f32)


def _rms(x, width):
    return lax.rsqrt(jnp.sum(x * x, axis=-1, keepdims=True) * (1.0 / width) + EPS)


def _layer_spec(shape, layer, n_grid):
    nd = len(shape)
    block = (None,) + tuple(shape[1:])
    if n_grid == 1:
        return pl.BlockSpec(block, lambda i: (layer,) + (0,) * (nd - 1))
    return pl.BlockSpec(block, lambda i, j: (layer,) + (0,) * (nd - 1))


def _rope(x, c_tab, s_tab):
    lane = lax.broadcasted_iota(jnp.int32, x.shape, 1)
    partner = jnp.where(lane < QK_ROPE // 2, pltpu.roll(x, LANES - QK_ROPE // 2, 1),
                        pltpu.roll(x, QK_ROPE // 2, 1))
    return x * c_tab + partner * s_tab


def _front_kernel(x_ref, ropec_ref, ropes_ref, ang_ref, win_ref, poolw_ref, pscale_ref, cdw_ref, cdwb_ref,
                  clng_ref, clnb_ref, cpw_ref, cpwb_ref, qag_ref, wuq_ref, kvag_ref, wukv_ref, qng_ref,
                  kng_ref, poutg_ref, coutg_ref,
                  mix_ref, q_ref, k_ref, v_ref, pool_ext, conv_ext):
    j = pl.program_id(1)
    tm = x_ref.shape[0]

    x = x_ref[...]
    h = x * _rms(x, D_MODEL) * ang_ref[...]
    z = _dot(h.astype(_bf16), win_ref[...])

    u = z[:, 0:POOL_W]

    @pl.when(j == 0)
    def _():
        pool_ext[0:POOL_HALO, :] = jnp.zeros((POOL_HALO, POOL_W), _f32)
        conv_ext[0:CONV_HALO, :] = jnp.zeros((CONV_HALO, CONV_W), _f32)

    @pl.when(j > 0)
    def _():
        pool_ext[0:POOL_HALO, :] = pool_ext[tm:tm + POOL_HALO, :]
        conv_ext[0:CONV_HALO, :] = conv_ext[tm:tm + CONV_HALO, :]

    pool_ext[POOL_HALO:POOL_HALO + tm, :] = u
    t_pos = (j * tm + lax.broadcasted_iota(jnp.int32, (tm, LANES), 0) + 1).astype(_f32)
    lane = lax.broadcasted_iota(jnp.int32, (tm, LANES), 1)
    pooled_halves = []
    for half, (w_lo, w_hi) in enumerate(((POOL_WINDOWS[0], POOL_WINDOWS[1]),
                                         (POOL_WINDOWS[2], POOL_WINDOWS[3]))):
        cols = slice(half * LANES, (half + 1) * LANES)
        s_lo = pool_ext[POOL_HALO:POOL_HALO + tm, cols]
        for d in range(1, w_lo):
            s_lo = s_lo + pool_ext[POOL_HALO - d:POOL_HALO - d + tm, cols]
        s_hi = s_lo
        for d in range(w_lo, w_hi):
            s_hi = s_hi + pool_ext[POOL_HALO - d:POOL_HALO - d + tm, cols]
        first = lane < POOL_GC
        win_sum = jnp.where(first, s_lo, s_hi)
        cnt = jnp.minimum(t_pos, jnp.where(first, float(w_lo), float(w_hi)))
        pooled_halves.append(win_sum / cnt - u[:, cols])
    pooled = jnp.concatenate(pooled_halves, axis=-1)
    y_pool = _dot(pooled.astype(_bf16), poolw_ref[...]) * pscale_ref[...]
    y_pool = y_pool * _rms(y_pool, POOL_W) * poutg_ref[...]

    a = z[:, POOL_W:POOL_W + CONV_W]
    gate = z[:, POOL_W + CONV_W:POOL_W + 2 * CONV_W]
    conv_ext[CONV_HALO:CONV_HALO + tm, :] = a * jax.nn.sigmoid(gate)
    base = CONV_HALO - (CONV_K - 1)
    acc = cdwb_ref[...]
    window = conv_ext[...]
    n_win = window.shape[0]
    for r in range(SUBLANES):
        tiles = [(base + kk) // SUBLANES for kk in range(CONV_K) if (base + kk) % SUBLANES == r]
        shifted = window if r == 0 else pltpu.roll(window, n_win - r, 0)
        for m in tiles:
            kk = m * SUBLANES + r - base
            acc = acc + shifted[m * SUBLANES:m * SUBLANES + tm, :] * cdw_ref[kk:kk + 1, :]
    mu = jnp.mean(acc, axis=-1, keepdims=True)
    cen = acc - mu
    var = jnp.mean(cen * cen, axis=-1, keepdims=True)
    ln = cen * lax.rsqrt(var + EPS) * clng_ref[...] + clnb_ref[...]
    sw = ln * jax.nn.sigmoid(ln)
    y_conv = _dot(sw.astype(_bf16), cpw_ref[...]) + cpwb_ref[...]
    y_conv = y_conv * _rms(y_conv, CONV_W) * coutg_ref[...]

    mix_ref[:, 0:POOL_W] = y_pool.astype(_bf16)
    mix_ref[:, POOL_W:POOL_W + CONV_W] = y_conv.astype(_bf16)

    s1 = POOL_W + 2 * CONV_W
    c_q = z[:, s1:s1 + Q_LORA]
    c_kv = z[:, s1 + Q_LORA:s1 + Q_LORA + KV_LORA]
    k_rope = z[:, s1 + Q_LORA + KV_LORA:s1 + Q_LORA + KV_LORA + LANES]
    c_tab = ropec_ref[...]
    s_tab = ropes_ref[...]

    q_all = _dot((c_q * _rms(c_q, Q_LORA) * qag_ref[...]).astype(_bf16), wuq_ref[...])
    kv_all = _dot((c_kv * _rms(c_kv, KV_LORA) * kvag_ref[...]).astype(_bf16), wukv_ref[...])
    sm_scale = math.log2(math.e) / math.sqrt(QK_HEAD)
    kr_ss = jnp.sum(k_rope * k_rope, axis=-1, keepdims=True)
    kr_rot = _rope(k_rope * kng_ref[:, LANES:2 * LANES], c_tab, s_tab)
    for hd in range(N_HEADS):
        qh = q_all[:, hd * HEAD_PAD:(hd + 1) * HEAD_PAD]
        qh = qh * (_rms(qh, QK_HEAD) * sm_scale) * qng_ref[...]
        q_ref[:, hd * HEAD_PAD:hd * HEAD_PAD + LANES] = qh[:, 0:LANES].astype(_bf16)
        q_ref[:, hd * HEAD_PAD + LANES:(hd + 1) * HEAD_PAD] = _rope(qh[:, LANES:], c_tab, s_tab).astype(_bf16)
        kn = kv_all[:, hd * QK_NOPE:(hd + 1) * QK_NOPE]
        k_rs = lax.rsqrt((jnp.sum(kn * kn, axis=-1, keepdims=True) + kr_ss) * (1.0 / QK_HEAD) + EPS)
        k_ref[:, hd * HEAD_PAD:hd * HEAD_PAD + LANES] = (kn * k_rs * kng_ref[:, 0:LANES]).astype(_bf16)
        k_ref[:, hd * HEAD_PAD + LANES:(hd + 1) * HEAD_PAD] = (kr_rot * k_rs).astype(_bf16)
        v_lo = hd * (V_DIM + LANES)
        v_ref[:, v_lo:v_lo + V_DIM] = kv_all[:, (N_HEADS + hd) * V_DIM:(N_HEADS + hd + 1) * V_DIM].astype(_bf16)
        v_ref[:, v_lo + V_DIM:v_lo + V_DIM + LANES] = jnp.ones((tm, LANES), _bf16)


def _mixer_front(layer, x, rope_c, rope_s, p):
    B, S, _ = x.shape
    tm = TM_FRONT
    row = lambda w: pl.BlockSpec((None, tm, w), lambda b, j: (b, j, 0))
    names = ("attn_norm_g", "w_in", "pool_w", "pool_scale", "conv_dw", "conv_dw_b", "conv_ln_g", "conv_ln_b",
             "conv_pw", "conv_pw_b", "q_a_g", "w_uq", "kv_a_g", "w_ukv", "q_norm_g", "k_norm_g",
             "pool_out_g", "conv_out_g")
    weights = [p[n] for n in names]
    in_specs = [row(D_MODEL), row(LANES), row(LANES)] + [_layer_spec(w.shape, layer, 2) for w in weights]
    out_shape = (jax.ShapeDtypeStruct((B, S, POOL_W + CONV_W), _bf16),
                 jax.ShapeDtypeStruct((B, S, N_HEADS * HEAD_PAD), _bf16),
                 jax.ShapeDtypeStruct((B, S, N_HEADS * HEAD_PAD), _bf16),
                 jax.ShapeDtypeStruct((B, S, N_HEADS * (V_DIM + LANES)), _bf16))
    out_specs = (row(POOL_W + CONV_W), row(N_HEADS * HEAD_PAD), row(N_HEADS * HEAD_PAD),
                 row(N_HEADS * (V_DIM + LANES)))
    return pl.pallas_call(
        _front_kernel,
        grid=(B, S // tm),
        in_specs=in_specs,
        out_specs=out_specs,
        out_shape=out_shape,
        scratch_shapes=[pltpu.VMEM((tm + POOL_HALO, POOL_W), _f32),
                        pltpu.VMEM((tm + CONV_HALO, CONV_W), _f32)],
        compiler_params=pltpu.CompilerParams(dimension_semantics=("arbitrary", "arbitrary"),
                                             vmem_limit_bytes=VMEM_LIMIT),
        name="mixer_front",
    )(x, rope_c, rope_s, *weights)


def _attn_kernel(q_ref, qnext_ref, k_ref, v_ref, o_ref, s_scr, m_scr):
    qi = pl.program_id(2)
    tq = q_ref.shape[0]
    n_tiles = k_ref.shape[0] // tq
    q_chunk = lax.broadcasted_iota(jnp.int32, (tq, tq), 0) // CHUNK
    k_chunk = lax.broadcasted_iota(jnp.int32, (tq, tq), 1) // CHUNK

    def score_phase(n, q_tile_ref):
        keys = (n + 1) * tq
        s = lax.dot_general(q_tile_ref[...], k_ref[0:keys, :], (((1,), (1,)), ((), ())),
                            preferred_element_type=_f32)
        diag = jnp.where(k_chunk <= q_chunk, s[:, n * tq:], NEG_BIG)
        s = diag if n == 0 else jnp.concatenate([s[:, :n * tq], diag], axis=-1)
        m_scr[n % 2] = jnp.broadcast_to(jnp.max(s, axis=-1, keepdims=True), (tq, LANES))
        s_scr[n % 2, :, 0:keys] = s

    def value_phase(n):
        keys = (n + 1) * tq
        p = jnp.exp2(s_scr[n % 2, :, 0:keys] - m_scr[n % 2, :, 0:1]).astype(_bf16)
        acc = _dot(p, v_ref[0:keys, :])
        o_ref[...] = (acc[:, 0:V_DIM] / acc[:, V_DIM:V_DIM + 1]).astype(o_ref.dtype)

    def step(n):
        if n == 0:
            score_phase(0, q_ref)
        value_phase(n)
        if n + 1 < n_tiles:
            score_phase(n + 1, qnext_ref)

    for n in range(n_tiles):
        pl.when(qi == n)(functools.partial(step, n))


def _attention(q, k, v):
    B, S, _ = q.shape
    nq = S // TQ
    return pl.pallas_call(
        _attn_kernel,
        grid=(B, N_HEADS, nq),
        in_specs=[pl.BlockSpec((None, TQ, HEAD_PAD), lambda b, h, i: (b, i, h)),
                  pl.BlockSpec((None, TQ, HEAD_PAD), lambda b, h, i: (b, jnp.minimum(i + 1, nq - 1), h)),
                  pl.BlockSpec((None, S, HEAD_PAD), lambda b, h, i: (b, 0, h)),
                  pl.BlockSpec((None, S, V_DIM + LANES), lambda b, h, i: (b, 0, h))],
        out_specs=pl.BlockSpec((None, TQ, V_DIM), lambda b, h, i: (b, i, h)),
        out_shape=jax.ShapeDtypeStruct((B, S, ATTN_W), _bf16),
        scratch_shapes=[pltpu.VMEM((2, TQ, S), _f32), pltpu.VMEM((2, TQ, LANES), _f32)],
        compiler_params=pltpu.CompilerParams(dimension_semantics=("arbitrary", "arbitrary", "arbitrary"),
                                             vmem_limit_bytes=VMEM_LIMIT),
        name="attention",
    )(q, q, k, v)


def _post_kernel(x_ref, mix_ref, att_ref, aog_ref, wo_ref, fng_ref, wr_ref, br_ref,
                 x1_ref, h2_ref, route_ref, rgate_ref, cnt_ref):
    tm = x_ref.shape[0]
    att = att_ref[...].astype(_f32)
    att_n = (att * _rms(att, ATTN_W) * aog_ref[...]).astype(_bf16)
    x1 = x_ref[...] + _dot(mix_ref[...], wo_ref[0:POOL_W + CONV_W, :]) + _dot(att_n, wo_ref[POOL_W + CONV_W:, :])
    x1_ref[...] = x1
    h2 = (x1 * _rms(x1, D_MODEL) * fng_ref[...]).astype(_bf16)
    h2_ref[...] = h2

    logits = _dot(h2, wr_ref[...]) + br_ref[...]
    lane = lax.broadcasted_iota(jnp.int32, logits.shape, 1)
    lane_f = lane.astype(_f32)
    work = logits
    sel = jnp.zeros(logits.shape, _f32)
    onehots, vals, idxs = [], [], []
    for _ in range(TOP_K):
        mx = jnp.max(work, axis=-1, keepdims=True)
        idx_f = jnp.min(jnp.where(work == mx, lane_f, float(LANES)), axis=-1, keepdims=True)
        idx = idx_f.astype(jnp.int32)
        hot = lane == idx
        onehots.append(hot)
        vals.append(mx)
        idxs.append(idx)
        sel = jnp.where(hot, 1.0, sel)
        work = jnp.where(hot, -jnp.inf, work)
    exps = [jnp.exp(vk - vals[0]) for vk in vals]
    denom = exps[0] + exps[1] + exps[2] + exps[3]

    nt = tm // ROUTE_TILE
    r_i = lax.broadcasted_iota(jnp.int32, (ROUTE_TILE, ROUTE_TILE), 0)
    c_i = lax.broadcasted_iota(jnp.int32, (ROUTE_TILE, ROUTE_TILE), 1)
    tri = jnp.where(c_i < r_i, 1.0, 0.0).astype(_bf16)
    upper = jnp.where(lax.broadcasted_iota(jnp.int32, (LANES, LANES), 0)
                      < lax.broadcasted_iota(jnp.int32, (LANES, LANES), 1), 1.0, 0.0).astype(_bf16)
    sel_b = sel.astype(_bf16)
    ranks, cnts = [], []
    for s in range(nt):
        rows = slice(s * ROUTE_TILE, (s + 1) * ROUTE_TILE)
        ranks.append(_dot(tri, sel_b[rows, :]))
        cnts.append(jnp.sum(sel[rows, :], axis=0, keepdims=True))
    cnt = jnp.concatenate(cnts + [jnp.zeros((cnt_ref.shape[0] - nt, LANES), _f32)], axis=0)
    cnt_ref[...] = cnt
    chunks = jnp.floor((cnt + (ROW_CHUNK - 1)) * (1.0 / ROW_CHUNK))
    boff = _dot(chunks.astype(_bf16), upper) * float(ROW_CHUNK)
    slot = jnp.concatenate(
        [ranks[s] + jnp.broadcast_to(boff[s:s + 1, :], (ROUTE_TILE, LANES)) for s in range(nt)], axis=0)

    route = jnp.zeros(logits.shape, jnp.int32)
    rgate = jnp.zeros(logits.shape, _f32)
    for kk in range(TOP_K):
        col = jnp.sum(jnp.where(onehots[kk], slot, 0.0), axis=-1, keepdims=True).astype(jnp.int32)
        route = jnp.where(lane == kk, idxs[kk], route)
        route = jnp.where(lane == TOP_K + kk, col, route)
        rgate = jnp.where(lane == kk, exps[kk] / denom, rgate)
    route_ref[...] = route
    rgate_ref[...] = rgate


def _post_mix(layer, x, mix, att, p):
    T = x.shape[0]
    tm = TM_POST
    row = lambda w: pl.BlockSpec((tm, w), lambda i: (i, 0))
    names = ("attn_out_g", "w_o", "ffn_norm_g", "w_router", "b_router")
    weights = [p[n] for n in names]
    out_shape = (jax.ShapeDtypeStruct((T, D_MODEL), _f32), jax.ShapeDtypeStruct((T, D_MODEL), _bf16),
                 jax.ShapeDtypeStruct((T, LANES), jnp.int32), jax.ShapeDtypeStruct((T, LANES), _f32),
                 jax.ShapeDtypeStruct((T // tm * SUBLANES, LANES), _f32))
    assert tm // ROUTE_TILE <= SUBLANES
    x1, h2, route, rgate, cnt = pl.pallas_call(
        _post_kernel,
        grid=(T // tm,),
        in_specs=[row(D_MODEL), row(POOL_W + CONV_W), row(ATTN_W)] + [_layer_spec(w.shape, layer, 1) for w in weights],
        out_specs=(row(D_MODEL), row(D_MODEL), row(LANES), row(LANES),
                   pl.BlockSpec((SUBLANES, LANES), lambda i: (i, 0))),
        out_shape=out_shape,
        compiler_params=pltpu.CompilerParams(dimension_semantics=("arbitrary",), vmem_limit_bytes=VMEM_LIMIT),
        name="post_mix",
    )(x, mix, att, *weights)
    cnt = cnt.reshape(T // tm, SUBLANES, LANES)[:, :tm // ROUTE_TILE].reshape(T // ROUTE_TILE, LANES)
    return x1, h2, route, rgate, cnt


N_SLAB = D_MODEL // LANES
assert N_SLAB == SUBLANES
CHUNK_SUBROWS = ROW_CHUNK * N_SLAB


def _to_token_major(ref, value):
    rows = value.shape[0]
    for j in range(N_SLAB):
        ref[pl.ds(j, rows, stride=N_SLAB), :] = value[:, j * LANES:(j + 1) * LANES]


def _from_token_major(ref, rows):
    return jnp.concatenate([ref[pl.ds(j, rows, stride=N_SLAB), :] for j in range(N_SLAB)], axis=-1)


def _subrow(row):
    return pl.multiple_of(row * N_SLAB, N_SLAB)


MAX_TILE_CHUNKS = ROUTE_TILE * TOP_K // ROW_CHUNK + N_EXPERTS
assert MAX_TILE_CHUNKS * ROW_CHUNK == STAGE_ROWS
DUMP_ROWS = STAGE_ROWS


ALWAYS_CHUNKS = ROUTE_TILE * TOP_K // ROW_CHUNK + N_EXPERTS // 4
CHUNK_GROUP = 8
assert (MAX_TILE_CHUNKS - ALWAYS_CHUNKS) % CHUNK_GROUP == 0


def _tile_chunks(chunk_rows, n_total, tile, chunk_fn):
    def run(lo, hi):
        for c in range(lo, hi):
            chunk_fn(c, c * ROW_CHUNK, chunk_rows[tile * MAX_TILE_CHUNKS + c])

    run(0, ALWAYS_CHUNKS)
    for lo in range(ALWAYS_CHUNKS, MAX_TILE_CHUNKS, CHUNK_GROUP):
        pl.when(n_total[tile] > lo)(functools.partial(run, lo, lo + CHUNK_GROUP))


def _dispatch_kernel(chunk_rows, n_total, pad_start, pad_chunks,
                     slot_ref, h2_ref, xs_hbm, stage, zeros, sems, zsem):
    i = pl.program_id(0)
    n_steps = pl.num_programs(0)
    cur = i % 2

    def chunk_copy(slot, stage_row, sorted_row):
        return pltpu.make_async_copy(stage.at[slot, pl.ds(_subrow(stage_row), CHUNK_SUBROWS)],
                                     xs_hbm.at[pl.ds(_subrow(sorted_row), CHUNK_SUBROWS)], sems.at[slot])

    def send_tile(tile, slot):
        _tile_chunks(chunk_rows, n_total, tile, lambda c, sr, dr: chunk_copy(slot, sr, dr).start(priority=c % 2))

    def wait_tile(tile, slot):
        _tile_chunks(chunk_rows, n_total, tile, lambda c, sr, dr: chunk_copy(slot, 0, 0).wait())

    @pl.when(i == 0)
    def _():
        zeros[...] = jnp.zeros(zeros.shape, _f32)

        def zero_copy(row):
            return pltpu.make_async_copy(zeros, xs_hbm.at[pl.ds(_subrow(row), CHUNK_SUBROWS)], zsem)

        for e in range(N_EXPERTS + 1):
            def start(c, carry, e=e):
                zero_copy(pad_start[e] + c * ROW_CHUNK).start()
                return carry
            lax.fori_loop(0, pad_chunks[e], start, 0)
        for e in range(N_EXPERTS + 1):
            def wait(c, carry, e=e):
                zero_copy(pad_start[e] + c * ROW_CHUNK).wait()
                return carry
            lax.fori_loop(0, pad_chunks[e], wait, 0)

    slots = slot_ref[...]
    row_i = lax.broadcasted_iota(jnp.int32, (STAGE_ROWS, ROUTE_TILE), 0)
    place = jnp.zeros((STAGE_ROWS, ROUTE_TILE), _f32)
    for kk in range(TOP_K):
        place = place + jnp.where(row_i == slots[kk:kk + 1, :], 1.0, 0.0)
    staged = _dot(place.astype(_bf16), h2_ref[...])
    _to_token_major(stage.at[cur], staged)

    @pl.when(i > 0)
    def _():
        wait_tile(i - 1, 1 - cur)

    send_tile(i, cur)

    @pl.when(i == n_steps - 1)
    def _():
        wait_tile(i, cur)


def _dispatch(h2, slots_t, sched, n_rows):
    T = h2.shape[0]
    grid_spec = pltpu.PrefetchScalarGridSpec(
        num_scalar_prefetch=4,
        grid=(T // ROUTE_TILE,),
        in_specs=[pl.BlockSpec((None, SUBLANES, ROUTE_TILE), lambda i, *_: (i, 0, 0)),
                  pl.BlockSpec((ROUTE_TILE, D_MODEL), lambda i, *_: (i, 0))],
        out_specs=pl.BlockSpec(memory_space=pl.ANY),
        scratch_shapes=[pltpu.VMEM((2, STAGE_ROWS * N_SLAB, LANES), _f32),
                        pltpu.VMEM((CHUNK_SUBROWS, LANES), _f32),
                        pltpu.SemaphoreType.DMA((2,)), pltpu.SemaphoreType.DMA(())],
    )
    return pl.pallas_call(
        _dispatch_kernel,
        grid_spec=grid_spec,
        out_shape=jax.ShapeDtypeStruct(((n_rows + DUMP_ROWS) * N_SLAB, LANES), _f32),
        compiler_params=pltpu.CompilerParams(dimension_semantics=("arbitrary",), vmem_limit_bytes=VMEM_LIMIT),
        name="dispatch",
    )(sched["write_rows"], sched["n_total"], sched["pad_start"], sched["pad_chunks"], slots_t, h2)


GU_GROUP = 2 * LANES


def _expert_kernel(blk_exp, blk_valid, xs_ref, wgu_ref, bgu_ref, wd_ref, bd_ref, y_ref, wgu_s, wd_s):
    i = pl.program_id(0)
    valid = blk_valid[i]

    @pl.when(jnp.logical_or(i == 0, blk_exp[i] != blk_exp[jnp.maximum(i - 1, 0)]))
    def _():
        r_i = lax.broadcasted_iota(jnp.int32, (GU_GROUP, GU_GROUP), 0)
        c_i = lax.broadcasted_iota(jnp.int32, (GU_GROUP, GU_GROUP), 1)
        src = jnp.where(c_i < LANES, 2 * c_i, 2 * (c_i - LANES) + 1)
        perm = jnp.where(r_i == src, 1.0, 0.0).astype(_bf16)
        for b in range(2 * D_FF // GU_GROUP):
            cols = slice(b * GU_GROUP, (b + 1) * GU_GROUP)
            wgu_s[:, cols] = _dot(wgu_ref[:, cols].astype(_bf16), perm).astype(_bf16)
        wd_s[...] = wd_ref[...].astype(_bf16)

    @pl.when(valid > 0)
    def _():
        x = _from_token_major(xs_ref, EXPERT_BLOCK).astype(_bf16)
        y = jnp.zeros((EXPERT_BLOCK, D_MODEL), _f32)
        groups = FF_CHUNK // LANES
        for c in range(D_FF // FF_CHUNK):
            lo = 2 * c * FF_CHUNK
            h = _dot(x, wgu_s[:, lo:lo + 2 * FF_CHUNK]) + bgu_ref[:, lo:lo + 2 * FF_CHUNK]
            hg = jnp.concatenate([h[:, (2 * q) * LANES:(2 * q + 1) * LANES] for q in range(groups)], axis=-1)
            hl = jnp.concatenate([h[:, (2 * q + 1) * LANES:(2 * q + 2) * LANES] for q in range(groups)], axis=-1)
            g = jnp.minimum(hg, SWIGLU_LIMIT)
            lin = jnp.clip(hl, -SWIGLU_LIMIT, SWIGLU_LIMIT)
            act = g * jax.nn.sigmoid(SWIGLU_ALPHA * g) * (lin + 1.0)
            y = y + _dot(act.astype(_bf16), wd_s[c * FF_CHUNK:(c + 1) * FF_CHUNK, :])
        _to_token_major(y_ref, y + bd_ref[...])

    @pl.when(valid == 0)
    def _():
        y_ref[...] = jnp.zeros(y_ref.shape, _f32)


def _experts(layer, xs, n_rows, blk_exp, blk_valid, w_gu, b_gu, w_down, b_down):
    nb = n_rows // EXPERT_BLOCK
    grid_spec = pltpu.PrefetchScalarGridSpec(
        num_scalar_prefetch=2,
        grid=(nb,),
        in_specs=[pl.BlockSpec((EXPERT_BLOCK * N_SLAB, LANES), lambda i, be, bv: (i, 0)),
                  pl.BlockSpec((None, None, D_MODEL, 2 * D_FF), lambda i, be, bv: (layer, be[i], 0, 0)),
                  pl.BlockSpec((None, None, 1, 2 * D_FF), lambda i, be, bv: (layer, be[i], 0, 0)),
                  pl.BlockSpec((None, None, D_FF, D_MODEL), lambda i, be, bv: (layer, be[i], 0, 0)),
                  pl.BlockSpec((None, None, 1, D_MODEL), lambda i, be, bv: (layer, be[i], 0, 0))],
        out_specs=pl.BlockSpec((EXPERT_BLOCK * N_SLAB, LANES), lambda i, be, bv: (i, 0)),
        scratch_shapes=[pltpu.VMEM((D_MODEL, 2 * D_FF), _bf16), pltpu.VMEM((D_FF, D_MODEL), _bf16)],
    )
    return pl.pallas_call(
        _expert_kernel,
        grid_spec=grid_spec,
        out_shape=jax.ShapeDtypeStruct((n_rows * N_SLAB, LANES), _f32),
        compiler_params=pltpu.CompilerParams(dimension_semantics=("arbitrary",), vmem_limit_bytes=VMEM_LIMIT),
        name="experts",
    )(blk_exp, blk_valid, xs, w_gu, b_gu, w_down, b_down)


def _combine_kernel(chunk_rows, n_total, x1_ref, route_ref, gate_ref, ys_hbm, out_ref, stage, sems):
    i = pl.program_id(0)
    n_steps = pl.num_programs(0)
    cur = i % 2

    def chunk_copy(slot, stage_row, sorted_row):
        return pltpu.make_async_copy(ys_hbm.at[pl.ds(_subrow(sorted_row), CHUNK_SUBROWS)],
                                     stage.at[slot, pl.ds(_subrow(stage_row), CHUNK_SUBROWS)], sems.at[slot])

    def fetch_tile(tile, slot):
        _tile_chunks(chunk_rows, n_total, tile, lambda c, sr, dr: chunk_copy(slot, sr, dr).start(priority=c % 2))

    def wait_tile(tile, slot):
        _tile_chunks(chunk_rows, n_total, tile, lambda c, sr, dr: chunk_copy(slot, 0, 0).wait())

    @pl.when(i == 0)
    def _():
        stage[...] = jnp.zeros(stage.shape, _f32)
        fetch_tile(0, 0)

    @pl.when(i + 1 < n_steps)
    def _():
        fetch_tile(i + 1, 1 - cur)

    wait_tile(i, cur)
    rows = _from_token_major(stage.at[cur], STAGE_ROWS).astype(_bf16)
    route = route_ref[...]
    gates = gate_ref[...]
    lane_i = lax.broadcasted_iota(jnp.int32, (ROUTE_TILE, STAGE_ROWS), 1)
    weight = jnp.zeros((ROUTE_TILE, STAGE_ROWS), _f32)
    for kk in range(TOP_K):
        weight = weight + jnp.where(lane_i == route[:, TOP_K + kk:TOP_K + kk + 1], gates[:, kk:kk + 1], 0.0)
    out_ref[...] = x1_ref[...] + _dot(weight.astype(_bf16), rows)


def _combine(x1, route, rgate, ys, sched):
    T = x1.shape[0]
    tc = ROUTE_TILE
    grid_spec = pltpu.PrefetchScalarGridSpec(
        num_scalar_prefetch=2,
        grid=(T // tc,),
        in_specs=[pl.BlockSpec((tc, D_MODEL), lambda i, *_: (i, 0)),
                  pl.BlockSpec((tc, LANES), lambda i, *_: (i, 0)),
                  pl.BlockSpec((tc, LANES), lambda i, *_: (i, 0)),
                  pl.BlockSpec(memory_space=pl.ANY)],
        out_specs=pl.BlockSpec((tc, D_MODEL), lambda i, *_: (i, 0)),
        scratch_shapes=[pltpu.VMEM((2, STAGE_ROWS * N_SLAB, LANES), _f32),
                        pltpu.SemaphoreType.DMA((2,))],
    )
    return pl.pallas_call(
        _combine_kernel,
        grid_spec=grid_spec,
        out_shape=jax.ShapeDtypeStruct((T, D_MODEL), _f32),
        compiler_params=pltpu.CompilerParams(dimension_semantics=("arbitrary",), vmem_limit_bytes=VMEM_LIMIT),
        name="combine",
    )(sched["read_rows"], sched["n_total"], x1, route, rgate, ys)


def _prepare_params(w):
    L = w["w_in"].shape[0]
    p = {}
    row = lambda a: a.reshape(L, 1, a.shape[-1])
    p["attn_norm_g"] = row(w["attn_norm_g"])
    p["w_in"] = jnp.pad(w["w_in"], ((0, 0), (0, 0), (0, D_IN_PAD - w["w_in"].shape[-1]))).astype(_bf16)
    eye = jnp.eye(len(POOL_WINDOWS), dtype=_f32)
    p["pool_w"] = jnp.einsum("lgcd,gh->lgchd", w["pool_w"], eye).reshape(L, POOL_W, POOL_W).astype(_bf16)
    p["pool_scale"] = row(w["pool_scale"])
    p["conv_dw"] = w["conv_dw"]
    p["conv_dw_b"] = row(w["conv_dw_b"])
    p["conv_ln_g"] = row(w["conv_ln_g"])
    p["conv_ln_b"] = row(w["conv_ln_b"])
    p["conv_pw"] = w["conv_pw"].astype(_bf16)
    p["conv_pw_b"] = row(w["conv_pw_b"])
    p["q_a_g"] = row(w["q_a_g"])
    wuq = w["w_uq"].reshape(L, Q_LORA, N_HEADS, QK_HEAD)
    p["w_uq"] = jnp.pad(wuq, ((0, 0), (0, 0), (0, 0), (0, HEAD_PAD - QK_HEAD))).reshape(
        L, Q_LORA, N_HEADS * HEAD_PAD).astype(_bf16)
    p["kv_a_g"] = row(w["kv_a_g"])
    wukv = w["w_ukv"].reshape(L, KV_LORA, N_HEADS, QK_NOPE + V_DIM)
    p["w_ukv"] = jnp.concatenate([wukv[..., :QK_NOPE].reshape(L, KV_LORA, N_HEADS * QK_NOPE),
                                  wukv[..., QK_NOPE:].reshape(L, KV_LORA, N_HEADS * V_DIM)], axis=-1).astype(_bf16)
    pad_head = lambda g: jnp.pad(g, ((0, 0), (0, HEAD_PAD - QK_HEAD))).reshape(L, 1, HEAD_PAD)
    p["q_norm_g"] = pad_head(w["q_norm_g"])
    p["k_norm_g"] = pad_head(w["k_norm_g"])
    p["pool_out_g"] = row(w["pool_out_g"])
    p["conv_out_g"] = row(w["conv_out_g"])
    p["attn_out_g"] = row(w["attn_out_g"])
    p["w_o"] = w["w_o"].astype(_bf16)
    p["ffn_norm_g"] = row(w["ffn_norm_g"])
    p["w_router"] = jnp.pad(w["w_router"], ((0, 0), (0, 0), (0, LANES - N_EXPERTS))).astype(_bf16)
    p["b_router"] = jnp.pad(w["b_router"], ((0, 0), (0, LANES - N_EXPERTS)),
                            constant_values=NEG_BIG).reshape(L, 1, LANES)
    E = w["w_gu"].shape[1]
    p["w_gu"] = w["w_gu"]
    bgu = w["b_gu"].reshape(L, E, 2 * D_FF // GU_GROUP, LANES, 2)
    p["b_gu"] = jnp.swapaxes(bgu, -1, -2).reshape(L, E, 1, 2 * D_FF)
    p["w_down"] = w["w_down"]
    p["b_down"] = w["b_down"].reshape(L, E, 1, D_MODEL)
    return p


def _rope_tables(positions):
    inv_freq = 1.0 / (ROPE_THETA ** (jnp.arange(0, QK_ROPE, 2, dtype=_f32) / QK_ROPE))
    ang = positions.astype(_f32)[..., None] * inv_freq
    cos, sin = jnp.cos(ang), jnp.sin(ang)
    zeros = jnp.zeros(cos.shape[:-1] + (LANES - QK_ROPE,), _f32)
    return (jnp.concatenate([cos, cos, zeros], axis=-1), jnp.concatenate([-sin, sin, zeros], axis=-1))


def _routing_schedule(cnt_f, n_tokens):
    n = cnt_f[:, 0:N_EXPERTS].astype(jnp.int32)
    counts = jnp.sum(n, axis=0)
    spare = ROW_CHUNK - 1
    padded = jnp.where(counts > 0, (counts + spare + EXPERT_BLOCK - 1) // EXPERT_BLOCK * EXPERT_BLOCK, 0)
    pend = jnp.cumsum(padded)
    pstart = pend - padded
    base = pstart[None, :] + jnp.cumsum(n, axis=0) - n
    n_chunks = (n + ROW_CHUNK - 1) // ROW_CHUNK
    max_rows = n_tokens * TOP_K + N_EXPERTS * (spare + EXPERT_BLOCK - 1) + ROW_CHUNK
    n_blocks = -(-max_rows // EXPERT_BLOCK)
    blk_row = jnp.arange(n_blocks, dtype=jnp.int32) * EXPERT_BLOCK
    blk_exp = jnp.minimum(jnp.sum((blk_row[:, None] >= pend[None, :]).astype(jnp.int32), axis=1), N_EXPERTS - 1)
    onehot = (blk_exp[:, None] == jnp.arange(N_EXPERTS, dtype=jnp.int32)[None, :]).astype(jnp.int32)
    seg_end = jnp.sum(onehot * (pstart + counts)[None, :], axis=1)
    blk_valid = jnp.where(blk_row < pend[-1], jnp.clip(seg_end - blk_row, 0, EXPERT_BLOCK), 0)
    i32 = lambda a: a.astype(jnp.int32)
    n_rows = n_blocks * EXPERT_BLOCK
    seg_chunks = (padded - counts + ROW_CHUNK - 1) // ROW_CHUNK
    pad_start = jnp.concatenate([pend - seg_chunks * ROW_CHUNK, pend[-1:]])
    pad_chunks = jnp.concatenate([seg_chunks, (n_rows + DUMP_ROWS - pend[-1:]) // ROW_CHUNK])
    cum_end = jnp.cumsum(n_chunks, axis=1)
    cum_start = cum_end - n_chunks
    c_idx = jnp.arange(MAX_TILE_CHUNKS, dtype=jnp.int32)
    e_of = jnp.minimum(jnp.sum((c_idx[None, :, None] >= cum_end[:, None, :]).astype(jnp.int32), axis=-1),
                       N_EXPERTS - 1)
    pick = (e_of[:, :, None] == jnp.arange(N_EXPERTS, dtype=jnp.int32)[None, None, :]).astype(jnp.int32)
    chunk_dst = (jnp.sum(pick * base[:, None, :], axis=-1)
                 + (c_idx[None, :] - jnp.sum(pick * cum_start[:, None, :], axis=-1)) * ROW_CHUNK)
    needed = c_idx[None, :] < cum_end[:, -1:]
    write_rows = jnp.where(needed, chunk_dst, n_rows + c_idx[None, :] * ROW_CHUNK)
    read_rows = jnp.where(needed, chunk_dst, 0)
    sched = dict(write_rows=i32(write_rows).reshape(-1), read_rows=i32(read_rows).reshape(-1),
                 n_total=i32(cum_end[:, -1]),
                 pad_start=i32(pad_start), pad_chunks=i32(pad_chunks))
    return sched, i32(blk_exp), i32(blk_valid), n_rows


def kernel(x, positions, attn_norm_g, w_in, pool_w, pool_scale, conv_dw, conv_dw_b, conv_ln_g, conv_ln_b, conv_pw, conv_pw_b, q_a_g, w_uq, kv_a_g, w_ukv, q_norm_g, k_norm_g, pool_out_g, conv_out_g, attn_out_g, w_o, ffn_norm_g, w_router, b_router, w_gu, b_gu, w_down, b_down):
    B, S, D = x.shape
    T = B * S
    depth = w_in.shape[0]
    p = _prepare_params(dict(
        attn_norm_g=attn_norm_g, w_in=w_in, pool_w=pool_w, pool_scale=pool_scale, conv_dw=conv_dw,
        conv_dw_b=conv_dw_b, conv_ln_g=conv_ln_g, conv_ln_b=conv_ln_b, conv_pw=conv_pw, conv_pw_b=conv_pw_b,
        q_a_g=q_a_g, w_uq=w_uq, kv_a_g=kv_a_g, w_ukv=w_ukv, q_norm_g=q_norm_g, k_norm_g=k_norm_g,
        pool_out_g=pool_out_g, conv_out_g=conv_out_g, attn_out_g=attn_out_g, w_o=w_o, ffn_norm_g=ffn_norm_g,
        w_router=w_router, b_router=b_router, w_gu=w_gu, b_gu=b_gu, w_down=w_down, b_down=b_down))
    rope_c, rope_s = _rope_tables(positions)
    for layer in range(depth):
        mix, q, k, v = _mixer_front(layer, x, rope_c, rope_s, p)
        att = _attention(q, k, v)
        x1, h2, route, rgate, cnt = _post_mix(layer, x.reshape(T, D), mix.reshape(T, -1), att.reshape(T, -1), p)
        sched, blk_exp, blk_valid, n_rows = _routing_schedule(cnt, T)
        slots_t = route[:, TOP_K:2 * TOP_K].reshape(T // ROUTE_TILE, ROUTE_TILE, TOP_K).transpose(0, 2, 1)
        slots_t = jnp.pad(slots_t, ((0, 0), (0, SUBLANES - TOP_K), (0, 0)), constant_values=-1)
        xs = _dispatch(h2, slots_t, sched, n_rows)
        ys = _experts(layer, xs, n_rows, blk_exp, blk_valid, p["w_gu"], p["b_gu"], p["w_down"], p["b_down"])
        x = _combine(x1, route, rgate, ys, sched).reshape(B, S, D)
    return x
```
